```python
import jax, jax.numpy as jnp
from jax import lax
import numpy as np

D_MODEL = 1024
BATCH = 4
SEQ = 4096
DEPTH = 4

GRID_W = 64
CTX_LEN = 256
EPS = 1e-6
N_MOD = 6
SGU_WIDTH = 1024
SGU_CHUNK = 128
SGU_GROUPS = 8
SGU_GROUP_DIM = SGU_WIDTH // SGU_GROUPS
GLA_HEADS = 4
GLA_KEY_DIM = D_MODEL // 2
GLA_VAL_DIM = D_MODEL
GLA_HEAD_K = GLA_KEY_DIM // GLA_HEADS
GLA_HEAD_V = GLA_VAL_DIM // GLA_HEADS
GLA_RANK = 16
GLA_GATE_NORMALIZER = 16.0
GLA_CHUNK = 64
D_FF = 2816
CONV_K = 3

IN_SPLITS = (SGU_WIDTH, SGU_WIDTH, GLA_KEY_DIM, GLA_VAL_DIM, D_MODEL, D_MODEL, GLA_KEY_DIM, GLA_VAL_DIM, 2 * GLA_RANK)
D_IN = sum(IN_SPLITS)
OFF_KVG = sum(IN_SPLITS[:6])

kernel_name = "hybrid_sgu_gla_convglu_prefix_dit"


def _split(z, sizes):
    offs = [int(o) for o in np.cumsum(sizes)[:-1]]
    return jnp.split(z, offs, axis=-1)


def _rms_norm(x, g):
    xf = x.astype(jnp.float32)
    y = xf * lax.rsqrt(jnp.mean(xf * xf, axis=-1, keepdims=True) + EPS)
    return (y * g.astype(jnp.float32)).astype(x.dtype)


def _layer_norm(x, g, b):
    xf = x.astype(jnp.float32)
    mu = jnp.mean(xf, axis=-1, keepdims=True)
    var = jnp.mean(jnp.square(xf - mu), axis=-1, keepdims=True)
    y = (xf - mu) * lax.rsqrt(var + EPS)
    return (y * g.astype(jnp.float32) + b.astype(jnp.float32)).astype(x.dtype)


def _modulate(h, shift, scale):
    return h * (1 + scale) + shift


def _spatial_gating(u, vs, ln_g, ln_b, w_s, b_s):
    bsz, length, _ = u.shape
    n = length // SGU_CHUNK
    vn = _layer_norm(vs, ln_g, ln_b).reshape(bsz, n, SGU_CHUNK, SGU_GROUPS, SGU_GROUP_DIM)
    s = jnp.einsum('gpq,bnqgc->bnpgc', w_s, vn) + b_s.T[:, :, None]
    return u * s.reshape(bsz, length, SGU_WIDTH)


def _heads(a, hd):
    return a.reshape(a.shape[0], a.shape[1], GLA_HEADS, hd).astype(jnp.float32)


def _log_decay(lr, w2, b2):
    z = (lr @ w2 + b2).astype(jnp.float32)
    la = jax.nn.log_sigmoid(z) / GLA_GATE_NORMALIZER
    return la.reshape(lr.shape[0], lr.shape[1], GLA_HEADS, GLA_HEAD_K)


def _gla_direction(q, k, v, log_a, s0):
    bsz, t, h, _ = k.shape
    n = t // GLA_CHUNK

    def chunks(a):
        return a.reshape(bsz, n, GLA_CHUNK, h, a.shape[-1])

    k, v, log_a = chunks(k), chunks(v), chunks(log_a)
    b = jnp.cumsum(log_a, axis=2)
    b_last = b[:, :, -1:]
    k_dec = k * jnp.exp(b_last - b)
    cm = lambda a: jnp.moveaxis(a, 1, 0)
    xs_state = (cm(k_dec), cm(v), cm(jnp.exp(b_last[:, :, 0])))

    if q is None:
        def step_state(s, xs):
            kd, vv, dec = xs
            return dec[..., None] * s + jnp.einsum('bchk,bchv->bhkv', kd, vv), None
        s_final, _ = lax.scan(step_state, s0, xs_state)
        return None, s_final

    q = chunks(q)
    q_dec = q * jnp.exp(b)
    k_inv = k * jnp.exp(-b)
    mask = jnp.tril(jnp.ones((GLA_CHUNK, GLA_CHUNK), dtype=bool))
    scores = jnp.where(mask, jnp.einsum('bnchk,bnshk->bnhcs', q_dec, k_inv), 0.0)
    o_intra = jnp.einsum('bnhcs,bnshv->bnchv', scores, v)

    def step(s, xs):
        qd, kd, vv, dec = xs
        o = jnp.einsum('bchk,bhkv->bchv', qd, s)
        return dec[..., None] * s + jnp.einsum('bchk,bchv->bhkv', kd, vv), o

    s_final, o_inter = lax.scan(step, s0, (cm(q_dec),) + xs_state)
    o = o_intra + jnp.moveaxis(o_inter, 0, 1)
    return o.reshape(bsz, t, h, -1), s_final


def _gla_bidir(q, k, v, lr, w2, b2, s0f, s0b):
    la_f = _log_decay(lr[..., :GLA_RANK], w2[0], b2[0])
    la_b = _log_decay(lr[..., GLA_RANK:], w2[1], b2[1])
    flip = lambda a: None if a is None else a[:, ::-1]
    o_f, s_f = _gla_direction(q, k, v, la_f, s0f)
    o_b, s_b = _gla_direction(flip(q), flip(k), flip(v), flip(la_b), s0b)
    o = None if q is None else o_f + flip(o_b)
    return o, s_f, s_b


def _token_mixer(h, w_in, sgu_ln_g, sgu_ln_b, sgu_w, sgu_b, gla_w2, gla_b2, gla_norm_g,
                 w_br_a, w_br_b, w_o, s0f, s0b):
    z = h @ w_in
    u, vs, q, r, ga, gb, k, v, lr = _split(z, IN_SPLITS)
    a = _spatial_gating(jax.nn.gelu(u, approximate=False), jax.nn.gelu(vs, approximate=False),
                        sgu_ln_g, sgu_ln_b, sgu_w, sgu_b)
    qh = _heads(q, GLA_HEAD_K) * (GLA_HEAD_K ** -0.5)
    o, s_f, s_b = _gla_bidir(qh, _heads(k, GLA_HEAD_K), _heads(v, GLA_HEAD_V), lr, gla_w2, gla_b2, s0f, s0b)
    o = o * lax.rsqrt(jnp.mean(o * o, axis=-1, keepdims=True) + EPS)
    o = o.reshape(h.shape[0], h.shape[1], GLA_VAL_DIM) * gla_norm_g.astype(jnp.float32)
    ob = o.astype(h.dtype) * jax.nn.silu(r)
    merged = jax.nn.sigmoid(ga) * (a @ w_br_a) + jax.nn.sigmoid(gb) * (ob @ w_br_b)
    return merged @ w_o, s_f, s_b


def _context_states(hc, w_in, gla_w2, gla_b2, s0):
    k, v, lr = _split(hc @ w_in[:, OFF_KVG:], IN_SPLITS[6:])
    _, s_f, s_b = _gla_bidir(None, _heads(k, GLA_HEAD_K), _heads(v, GLA_HEAD_V), lr, gla_w2, gla_b2, s0, s0)
    return s_f, s_b


def _conv_ffn(h, rows, w_up, conv_w, conv_b, w_down):
    bsz, length, _ = h.shape
    a, val = _split(h @ w_up, (D_FF, D_FF))
    img = a.reshape(bsz, rows, length // rows, D_FF)
    a = lax.conv_general_dilated(img, conv_w[:, :, None, :], (1, 1), 'SAME',
                                 dimension_numbers=('NHWC', 'HWIO', 'NHWC'),
                                 feature_group_count=D_FF) + conv_b
    return (jax.nn.gelu(a.reshape(bsz, length, D_FF), approximate=False) * val) @ w_down


def setup_inputs(seed: int = 0) -> dict:
    key = jax.random.key(seed)
    ks = jax.random.split(key, 32)

    def nrm(k, shape, scale):
        return jax.random.normal(k, shape, jnp.float32) * scale

    def gain(k, shape):
        return 1.0 + nrm(k, shape, 0.1)

    D = D_MODEL
    return {
        "x": nrm(ks[0], (BATCH, SEQ, D), 1.0),
        "c": nrm(ks[1], (BATCH, D), 1.0),
        "ctx": nrm(ks[2], (BATCH, CTX_LEN, D), 1.0),
        "c_ctx": nrm(ks[3], (D,), 1.0),
        "w_ada": nrm(ks[4], (DEPTH, D, N_MOD * D), 0.5 * D ** -0.5),
        "b_ada": nrm(ks[5], (DEPTH, N_MOD * D), 0.01),
        "norm1_g": gain(ks[6], (DEPTH, D)),
        "norm2_g": gain(ks[7], (DEPTH, D)),
        "w_in": nrm(ks[8], (DEPTH, D, D_IN), D ** -0.5),
        "sgu_ln_g": gain(ks[9], (DEPTH, SGU_WIDTH)),
        "sgu_ln_b": nrm(ks[10], (DEPTH, SGU_WIDTH), 0.01),
        "sgu_w": nrm(ks[11], (DEPTH, SGU_GROUPS, SGU_CHUNK, SGU_CHUNK), 0.5 * SGU_CHUNK ** -0.5),
        "sgu_b": gain(ks[12], (DEPTH, SGU_GROUPS, SGU_CHUNK)),
        "gla_w2": nrm(ks[13], (DEPTH, 2, GLA_RANK, GLA_KEY_DIM), GLA_RANK ** -0.5),
        "gla_b2": nrm(ks[14], (DEPTH, 2, GLA_KEY_DIM), 0.1),
        "gla_norm_g": gain(ks[15], (DEPTH, GLA_VAL_DIM)),
        "w_br_a": nrm(ks[16], (DEPTH, SGU_WIDTH, D), SGU_WIDTH ** -0.5),
        "w_br_b": nrm(ks[17], (DEPTH, GLA_VAL_DIM, D), GLA_VAL_DIM ** -0.5),
        "w_o": nrm(ks[18], (DEPTH, D, D), D ** -0.5),
        "ffn_w_up": nrm(ks[19], (DEPTH, D, 2 * D_FF), D ** -0.5),
        "ffn_conv_w": nrm(ks[20], (DEPTH, CONV_K, CONV_K, D_FF), 1.0 / CONV_K),
        "ffn_conv_b": nrm(ks[21], (DEPTH, D_FF), 0.01),
        "ffn_w_down": nrm(ks[22], (DEPTH, D_FF, D), D_FF ** -0.5),
        "final_norm_g": gain(ks[23], (D,)),
    }


def reference(x, c, ctx, c_ctx, w_ada, b_ada, norm1_g, norm2_g, w_in, sgu_ln_g, sgu_ln_b, sgu_w, sgu_b,
              gla_w2, gla_b2, gla_norm_g, w_br_a, w_br_b, w_o, ffn_w_up, ffn_conv_w, ffn_conv_b,
              ffn_w_down, final_norm_g):
    bsz, seq, _ = x.shape
    rows = seq // GRID_W
    s0 = jnp.zeros((bsz, GLA_HEADS, GLA_HEAD_K, GLA_HEAD_V), jnp.float32)
    for l in range(DEPTH):
        last = l == DEPTH - 1
        sh1x, sc1x, g1x, sh2x, sc2x, g2x = jnp.split((jax.nn.silu(c) @ w_ada[l] + b_ada[l])[:, None, :], N_MOD, axis=-1)
        sh1c, sc1c, g1c, sh2c, sc2c, g2c = jnp.split(jax.nn.silu(c_ctx) @ w_ada[l] + b_ada[l], N_MOD, axis=-1)
        tm = (w_in[l], sgu_ln_g[l], sgu_ln_b[l], sgu_w[l], sgu_b[l], gla_w2[l], gla_b2[l], gla_norm_g[l],
              w_br_a[l], w_br_b[l], w_o[l])
        hc = _modulate(_rms_norm(ctx, norm1_g[l]), sh1c, sc1c)
        if last:
            s_f, s_b = _context_states(hc, w_in[l], gla_w2[l], gla_b2[l], s0)
        else:
            y_c, s_f, s_b = _token_mixer(hc, *tm, s0, s0)
        hx = _modulate(_rms_norm(x, norm1_g[l]), sh1x, sc1x)
        y_x, _, _ = _token_mixer(hx, *tm, s_f, s_b)
        x = x + g1x * y_x
        hx2 = _modulate(_rms_norm(x, norm2_g[l]), sh2x, sc2x)
        x = x + g2x * _conv_ffn(hx2, rows, ffn_w_up[l], ffn_conv_w[l], ffn_conv_b[l], ffn_w_down[l])
        if not last:
            ctx = ctx + g1c * y_c
            hc2 = _modulate(_rms_norm(ctx, norm2_g[l]), sh2c, sc2c)
            ctx = ctx + g2c * _conv_ffn(hc2, 1, ffn_w_up[l], ffn_conv_w[l], ffn_conv_b[l], ffn_w_down[l])
    return _rms_norm(x, final_norm_g)
```

```python
import functools

import jax
import jax.numpy as jnp
from jax import lax
from jax.experimental import pallas as pl
from jax.experimental.pallas import tpu as pltpu

F32 = jnp.float32
BF16 = jnp.bfloat16

EPS = 1e-6
N_MOD = 6
SGU_CHUNK = 128
SGU_GROUPS = 8
GLA_HEADS = 4
GLA_RANK = 16
GLA_GATE_NORMALIZER = 16.0
GLA_CHUNK = 64
GRID_W = 64
CONV_K = 3
LANE = 128
SUBLANE = 8
VMEM_LIMIT_BYTES = 56 * 1024 * 1024
INV_SQRT2 = 0.7071067811865476


def _params(*semantics):
    return pltpu.CompilerParams(dimension_semantics=semantics, vmem_limit_bytes=VMEM_LIMIT_BYTES)


def _const_spec(shape):
    zeros = (0,) * len(shape)
    return pl.BlockSpec(shape, lambda *_: zeros, pipeline_mode=pl.Buffered(1))


def _gelu(v):
    return 0.5 * v * (1.0 + lax.erf(v * INV_SQRT2))


def _sigmoid(v):
    return 1.0 / (1.0 + jnp.exp(-v))


def _log_sigmoid(v):
    return jnp.minimum(v, 0.0) - jnp.log1p(jnp.exp(-jnp.abs(v)))


def _rms_mod(x, g, shift, scale):
    ms = jnp.mean(x * x, axis=-1, keepdims=True)
    y = x * lax.rsqrt(ms + EPS) * g
    return y * (1.0 + scale) + shift


def _dot(a, b):
    return jnp.dot(a, b, preferred_element_type=F32)


def _ada_kernel(c_ref, w_ref, b_ref, o_ref):
    c = c_ref[...]
    s = (c * _sigmoid(c)).astype(BF16)
    o_ref[0] = _dot(s, w_ref[0].astype(BF16)) + b_ref[0]


def _ada_mods(cc, w_ada, b_ada):
    depth, d, n = w_ada.shape
    tn = n // 4
    return pl.pallas_call(
        _ada_kernel,
        out_shape=jax.ShapeDtypeStruct((depth, cc.shape[0], n), F32),
        grid=(depth, n // tn),
        in_specs=[
            pl.BlockSpec(cc.shape, lambda l, j: (0, 0)),
            pl.BlockSpec((1, d, tn), lambda l, j: (l, 0, j)),
            pl.BlockSpec((1, 1, tn), lambda l, j: (l, 0, j)),
        ],
        out_specs=pl.BlockSpec((1, cc.shape[0], tn), lambda l, j: (l, 0, j)),
        compiler_params=_params("parallel", "parallel"),
        name="ada_mods",
    )(cc, w_ada, b_ada.reshape(depth, 1, n))


def _inproj_kernel(x_ref, sh_ref, sc_ref, g_ref, w_ref, wlr_ref, w2_ref, b2_ref, lng_ref, lnb_ref,
                   gu_ref, vn_ref, q_ref, k_ref, v_ref, rs_ref, sga_ref, sgb_ref, la_ref,
                   *, d, kd, q_scale):
    hb = _rms_mod(x_ref[0], g_ref[...], sh_ref[0], sc_ref[0]).astype(BF16)

    def seg(lo, width):
        return _dot(hb, w_ref[:, lo:lo + width])

    off = 0
    gu_ref[0] = _gelu(seg(off, d)).astype(BF16)
    off += d
    gv = _gelu(seg(off, d))
    mu = jnp.mean(gv, axis=-1, keepdims=True)
    cen = gv - mu
    var = jnp.mean(cen * cen, axis=-1, keepdims=True)
    vn_ref[0] = (cen * lax.rsqrt(var + EPS) * lng_ref[...] + lnb_ref[...]).astype(BF16)
    off += d
    q_ref[0] = seg(off, kd) * q_scale
    off += kd
    r = seg(off, d)
    rs_ref[0] = (r * _sigmoid(r)).astype(BF16)
    off += d
    sga_ref[0] = _sigmoid(seg(off, d)).astype(BF16)
    off += d
    sgb_ref[0] = _sigmoid(seg(off, d)).astype(BF16)
    off += d
    k_ref[0] = seg(off, kd)
    off += kd
    v_ref[0] = seg(off, d).astype(BF16)
    lr = _dot(hb, wlr_ref[...]).astype(BF16)
    z = _dot(lr, w2_ref[...]) + b2_ref[...]
    la_ref[0] = _log_sigmoid(z) * (1.0 / GLA_GATE_NORMALIZER)


def _in_proj(x, shift, scale, norm_g, w_main, w_lr, w2cat, b2cat, ln_g, ln_b, *, tm):
    bsz, length, d = x.shape
    kd = d // 2
    hk = kd // GLA_HEADS
    row = lambda b, i: (b, i, 0)
    mod = lambda b, i: (b, 0, 0)
    bf = lambda n: jax.ShapeDtypeStruct((bsz, length, n), BF16)
    f32 = lambda n: jax.ShapeDtypeStruct((bsz, length, n), F32)
    out_shape = (bf(d), bf(d), f32(kd), f32(kd), bf(d), bf(d), bf(d), bf(d), f32(2 * kd))
    out_specs = tuple(pl.BlockSpec((1, tm, s.shape[-1]), row) for s in out_shape)
    return pl.pallas_call(
        functools.partial(_inproj_kernel, d=d, kd=kd, q_scale=float(hk) ** -0.5),
        out_shape=out_shape,
        grid=(bsz, length // tm),
        in_specs=[
            pl.BlockSpec((1, tm, d), row),
            pl.BlockSpec((1, 1, d), mod),
            pl.BlockSpec((1, 1, d), mod),
            _const_spec((1, d)),
            _const_spec(w_main.shape),
            _const_spec(w_lr.shape),
            _const_spec(w2cat.shape),
            _const_spec(b2cat.shape),
            _const_spec((1, d)),
            _const_spec((1, d)),
        ],
        out_specs=out_specs,
        compiler_params=_params("parallel", "parallel"),
        name="in_proj",
    )(x, shift, scale, norm_g, w_main, w_lr, w2cat, b2cat, ln_g, ln_b)


def _split3(v):
    hi = v.astype(BF16)
    r1 = v - hi.astype(F32)
    mid = r1.astype(BF16)
    lo = (r1 - mid.astype(F32)).astype(BF16)
    return hi, mid, lo


def _gla_direction_chunk(q_ref, k_ref, v_ref, la_ref, o_ref, st_ref, rows, ones_tri, keep, last_row, *, hk, hv):
    hi, mid, lo = _split3(la_ref[0, rows, :])
    cum = _dot(ones_tri, hi) + _dot(ones_tri, mid) + _dot(ones_tri, lo)
    for h in range(GLA_HEADS):
        ks = slice(h * hk, (h + 1) * hk)
        vs = slice(h * hv, (h + 1) * hv)
        b = cum[:, ks]
        b_last = b[last_row:last_row + 1, :]
        q = q_ref[0, rows, ks]
        k = k_ref[0, rows, ks]
        v = v_ref[0, rows, vs]
        q_dec = (q * jnp.exp(b)).astype(BF16)
        k_inv = (k * jnp.exp(-b)).astype(BF16)
        k_dec = (k * jnp.exp(b_last - b)).astype(BF16)
        scores = lax.dot_general(q_dec, k_inv, (((1,), (1,)), ((), ())), preferred_element_type=F32)
        scores = jnp.where(keep, scores, 0.0).astype(BF16)
        state = st_ref[0, h]
        o_inter = lax.dot_general(q_dec, state.astype(BF16), (((1,), (1,)), ((), ())), preferred_element_type=F32)
        o_ref[0, rows, vs] = _dot(scores, v) + o_inter
        kv = lax.dot_general(v, k_dec, (((0,), (0,)), ((), ())), preferred_element_type=F32)
        st_ref[0, h] = state * jnp.exp(b_last) + kv


def _gla_kernel(qf_ref, kf_ref, vf_ref, laf_ref, qb_ref, kb_ref, vb_ref, lab_ref, s0f_ref, s0b_ref,
                of_ref, ob_ref, sf_ref, sb_ref, *, n_chunks, hk, hv):
    @pl.when(pl.program_id(1) == 0)
    def _():
        sf_ref[...] = s0f_ref[...]
        sb_ref[...] = s0b_ref[...]

    r = lax.broadcasted_iota(jnp.int32, (GLA_CHUNK, GLA_CHUNK), 0)
    c = lax.broadcasted_iota(jnp.int32, (GLA_CHUNK, GLA_CHUNK), 1)
    lower = c <= r
    upper = c >= r
    ones_lower = lower.astype(F32).astype(BF16)
    ones_upper = upper.astype(F32).astype(BF16)

    def body(ci, carry):
        rows_f = pl.ds(pl.multiple_of(ci * GLA_CHUNK, GLA_CHUNK), GLA_CHUNK)
        rows_b = pl.ds(pl.multiple_of((n_chunks - 1 - ci) * GLA_CHUNK, GLA_CHUNK), GLA_CHUNK)
        _gla_direction_chunk(qf_ref, kf_ref, vf_ref, laf_ref, of_ref, sf_ref, rows_f, ones_lower, lower,
                             GLA_CHUNK - 1, hk=hk, hv=hv)
        _gla_direction_chunk(qb_ref, kb_ref, vb_ref, lab_ref, ob_ref, sb_ref, rows_b, ones_upper, upper,
                             0, hk=hk, hv=hv)
        return carry

    lax.fori_loop(0, n_chunks, body, 0)


def _gla(q, k, v, la, s0f, s0b, *, tile):
    bsz, length, kd = q.shape
    vd = v.shape[-1]
    hk, hv = kd // GLA_HEADS, vd // GLA_HEADS
    nt = length // tile
    fwd = lambda b, i: (b, i, 0)
    bwd = lambda b, i: (b, nt - 1 - i, 0)
    bwd_la = lambda b, i: (b, nt - 1 - i, 1)
    st = lambda b, i: (b, 0, 0, 0)
    st_spec = pl.BlockSpec((1, GLA_HEADS, hv, hk), st)
    return pl.pallas_call(
        functools.partial(_gla_kernel, n_chunks=tile // GLA_CHUNK, hk=hk, hv=hv),
        out_shape=(
            jax.ShapeDtypeStruct((bsz, length, vd), F32),
            jax.ShapeDtypeStruct((bsz, length, vd), F32),
            jax.ShapeDtypeStruct(s0f.shape, F32),
            jax.ShapeDtypeStruct(s0b.shape, F32),
        ),
        grid=(bsz, nt),
        in_specs=[
            pl.BlockSpec((1, tile, kd), fwd),
            pl.BlockSpec((1, tile, kd), fwd),
            pl.BlockSpec((1, tile, vd), fwd),
            pl.BlockSpec((1, tile, kd), fwd),
            pl.BlockSpec((1, tile, kd), bwd),
            pl.BlockSpec((1, tile, kd), bwd),
            pl.BlockSpec((1, tile, vd), bwd),
            pl.BlockSpec((1, tile, kd), bwd_la),
            st_spec,
            st_spec,
        ],
        out_specs=(
            pl.BlockSpec((1, tile, vd), fwd),
            pl.BlockSpec((1, tile, vd), bwd),
            st_spec,
            st_spec,
        ),
        compiler_params=_params("parallel", "arbitrary"),
        name="gla_scan",
    )(q, k, v, la, q, k, v, la, s0f, s0b)


def _mix_kernel(x_ref, gu_ref, vn_ref, of_ref, ob_ref, rs_ref, sga_ref, sgb_ref, ws_ref, bs_ref, gn_ref,
                wa_ref, wb_ref, wo_ref, g1_ref, n2_ref, sh2_ref, sc2_ref, x1_ref, h2_ref, s_buf,
                *, tm, hv):
    for n in range(tm // SGU_CHUNK):
        rows = slice(n * SGU_CHUNK, (n + 1) * SGU_CHUNK)
        for g in range(SGU_GROUPS):
            cols = slice(g * LANE, (g + 1) * LANE)
            s_buf[rows, cols] = _dot(ws_ref[g], vn_ref[0, rows, cols]) + bs_ref[:, cols]
    a = (gu_ref[0].astype(F32) * s_buf[...]).astype(BF16)
    proj_a = _dot(a, wa_ref[...])
    o = of_ref[0] + ob_ref[0]
    for h in range(GLA_HEADS):
        cols = slice(h * hv, (h + 1) * hv)
        oh = o[:, cols]
        ms = jnp.mean(oh * oh, axis=-1, keepdims=True)
        s_buf[:, cols] = oh * lax.rsqrt(ms + EPS) * gn_ref[:, cols]
    ob = (s_buf[...] * rs_ref[0].astype(F32)).astype(BF16)
    proj_b = _dot(ob, wb_ref[...])
    merged = (sga_ref[0].astype(F32) * proj_a + sgb_ref[0].astype(F32) * proj_b).astype(BF16)
    x1 = x_ref[0] + g1_ref[0] * _dot(merged, wo_ref[...])
    x1_ref[0] = x1
    h2_ref[0] = _rms_mod(x1, n2_ref[...], sh2_ref[0], sc2_ref[0]).astype(BF16)


def _mix(x, gu, vn, o_f, o_b, rs, sga, sgb, ws, bs_full, gn_g, wa, wb, wo, g1, n2_g, sh2, sc2, *, tm):
    bsz, length, d = x.shape
    row = lambda b, i: (b, i, 0)
    mod = lambda b, i: (b, 0, 0)
    tile = pl.BlockSpec((1, tm, d), row)
    modv = pl.BlockSpec((1, 1, d), mod)
    return pl.pallas_call(
        functools.partial(_mix_kernel, tm=tm, hv=d // GLA_HEADS),
        out_shape=(jax.ShapeDtypeStruct((bsz, length, d), F32), jax.ShapeDtypeStruct((bsz, length, d), BF16)),
        grid=(bsz, length // tm),
        in_specs=[tile] * 8 + [
            _const_spec(ws.shape),
            _const_spec(bs_full.shape),
            _const_spec((1, d)),
            _const_spec(wa.shape),
            _const_spec(wb.shape),
            _const_spec(wo.shape),
            modv,
            _const_spec((1, d)),
            modv,
            modv,
        ],
        out_specs=(tile, tile),
        scratch_shapes=[pltpu.VMEM((tm, d), F32)],
        compiler_params=_params("parallel", "parallel"),
        name="mix",
    )(x, gu, vn, o_f, o_b, rs, sga, sgb, ws, bs_full, gn_g, wa, wb, wo, g1, n2_g, sh2, sc2)


def _ffn_up_kernel(h_ref, w_ref, a_ref, val_ref, *, dff):
    hb = h_ref[0]
    a_ref[0] = _dot(hb, w_ref[:, :dff]).astype(BF16)
    val_ref[0] = _dot(hb, w_ref[:, dff:]).astype(BF16)


def _ffn_up(h2, w_up, *, tm):
    bsz, length, d = h2.shape
    dff = w_up.shape[1] // 2
    row = lambda b, i: (b, i, 0)
    out = jax.ShapeDtypeStruct((bsz, length, dff), BF16)
    return pl.pallas_call(
        functools.partial(_ffn_up_kernel, dff=dff),
        out_shape=(out, out),
        grid=(bsz, length // tm),
        in_specs=[pl.BlockSpec((1, tm, d), row), _const_spec(w_up.shape)],
        out_specs=(pl.BlockSpec((1, tm, dff), row), pl.BlockSpec((1, tm, dff), row)),
        compiler_params=_params("parallel", "parallel"),
        name="ffn_up",
    )(h2, w_up)


def _ffn_down_kernel(*refs, tm, img_w, has_rows, final_norm):
    if has_rows:
        a_ref, prev_ref, next_ref = refs[:3]
        refs = refs[3:]
    else:
        a_ref = refs[0]
        refs = refs[1:]
    val_ref, cw_ref, cb_ref, wd_ref, x_ref, g2_ref, fg_ref, o_ref, pad, gated = refs
    dff = a_ref.shape[-1]
    base = img_w if has_rows else 0
    i = pl.program_id(1)

    if has_rows:
        first = i == 0
        last = i == pl.num_programs(1) - 1
        pad[0:base, :] = jnp.where(first, 0.0, prev_ref[0].astype(F32))
        pad[base + tm:base + tm + img_w, :] = jnp.where(last, 0.0, next_ref[0].astype(F32))
    pad[base:base + tm, :] = a_ref[0].astype(F32)

    col = lax.broadcasted_iota(jnp.int32, (img_w, LANE), 0)
    not_first_col = col >= 1
    not_last_col = col <= img_w - 2
    drs = (-1, 0, 1) if has_rows else (0,)

    def row_body(r, carry):
        rows = pl.ds(pl.multiple_of(r * img_w, img_w), img_w)
        for s in range(dff // LANE):
            cols = slice(s * LANE, (s + 1) * LANE)
            acc = [None, None, None]
            for dr in drs:
                src = pad[pl.ds(pl.multiple_of(base + (r + dr) * img_w, img_w), img_w), cols]
                for dc in (-1, 0, 1):
                    tap = cw_ref[(dr + 1) * CONV_K + dc + 1:(dr + 1) * CONV_K + dc + 2, cols]
                    acc[dc + 1] = tap * src if acc[dc + 1] is None else acc[dc + 1] + tap * src
            left = jnp.where(not_first_col, pltpu.roll(acc[0], 1, 0), 0.0)
            right = jnp.where(not_last_col, pltpu.roll(acc[2], img_w - 1, 0), 0.0)
            conv = acc[1] + left + right + cb_ref[:, cols]
            gated[rows, cols] = (_gelu(conv) * val_ref[0, rows, cols].astype(F32)).astype(BF16)
        return carry

    lax.fori_loop(0, tm // img_w, row_body, 0)
    x2 = x_ref[0] + g2_ref[0] * _dot(gated[...], wd_ref[...])
    if final_norm:
        ms = jnp.mean(x2 * x2, axis=-1, keepdims=True)
        x2 = x2 * lax.rsqrt(ms + EPS) * fg_ref[...]
    o_ref[0] = x2


def _ffn_down(a, val, conv_w, conv_b, w_down, x1, g2, final_g, *, tm, img_w, final_norm):
    bsz, length, dff = a.shape
    d = x1.shape[-1]
    has_rows = length > img_w
    rows_per_tile = tm // img_w
    n_rows = length // img_w
    row = lambda b, i: (b, i, 0)
    mod = lambda b, i: (b, 0, 0)
    in_specs = [pl.BlockSpec((1, tm, dff), row)]
    args = [a]
    if has_rows:
        in_specs += [
            pl.BlockSpec((1, img_w, dff), lambda b, i: (b, jnp.maximum(i * rows_per_tile - 1, 0), 0)),
            pl.BlockSpec((1, img_w, dff), lambda b, i: (b, jnp.minimum((i + 1) * rows_per_tile, n_rows - 1), 0)),
        ]
        args += [a, a]
    in_specs += [
        pl.BlockSpec((1, tm, dff), row),
        _const_spec(conv_w.shape),
        _const_spec(conv_b.shape),
        _const_spec(w_down.shape),
        pl.BlockSpec((1, tm, d), row),
        pl.BlockSpec((1, 1, d), mod),
        _const_spec((1, d)),
    ]
    args += [val, conv_w, conv_b, w_down, x1, g2, final_g]
    pad_rows = tm + (2 * img_w if has_rows else 0)
    return pl.pallas_call(
        functools.partial(_ffn_down_kernel, tm=tm, img_w=img_w, has_rows=has_rows, final_norm=final_norm),
        out_shape=jax.ShapeDtypeStruct(x1.shape, F32),
        grid=(bsz, length // tm),
        in_specs=in_specs,
        out_specs=pl.BlockSpec((1, tm, d), row),
        scratch_shapes=[pltpu.VMEM((pad_rows, dff), F32), pltpu.VMEM((tm, dff), BF16)],
        compiler_params=_params("parallel", "parallel"),
        name="ffn_down",
    )(*args)


def _stream_layer(x, mods, wts, s0f, s0b, *, tm, gla_tile, ffn_tile, img_w, states_only, final_norm):
    sh1, sc1, g1, sh2, sc2, g2 = mods
    gu, vn, q, k, v, rs, sga, sgb, la = _in_proj(
        x, sh1, sc1, wts["norm1_g"], wts["w_main"], wts["w_lr"], wts["w2cat"], wts["b2cat"],
        wts["sgu_ln_g"], wts["sgu_ln_b"], tm=tm)
    o_f, o_b, s_f, s_b = _gla(q, k, v, la, s0f, s0b, tile=gla_tile)
    if states_only:
        return None, s_f, s_b
    x1, h2 = _mix(x, gu, vn, o_f, o_b, rs, sga, sgb, wts["sgu_w"], wts["sgu_b_full"], wts["gla_norm_g"],
                  wts["w_br_a"], wts["w_br_b"], wts["w_o"], g1, wts["norm2_g"], sh2, sc2, tm=tm)
    a, val = _ffn_up(h2, wts["ffn_w_up"], tm=tm)
    x2 = _ffn_down(a, val, wts["ffn_conv_w"], wts["ffn_conv_b"], wts["ffn_w_down"], x1, g2, wts["final_g"],
                   tm=ffn_tile, img_w=img_w, final_norm=final_norm)
    return x2, s_f, s_b


def kernel(x, c, ctx, c_ctx, w_ada, b_ada, norm1_g, norm2_g, w_in, sgu_ln_g, sgu_ln_b, sgu_w, sgu_b, gla_w2, gla_b2, gla_norm_g, w_br_a, w_br_b, w_o, ffn_w_up, ffn_conv_w, ffn_conv_b, ffn_w_down, final_norm_g):
    bsz, seq, d = x.shape
    ctx_len = ctx.shape[1]
    depth = w_in.shape[0]
    kd = d // 2
    hk, hv = kd // GLA_HEADS, d // GLA_HEADS
    n_main = w_in.shape[-1] - 2 * GLA_RANK
    dff = ffn_w_down.shape[1]

    n_rows = -(-(bsz + 1) // SUBLANE) * SUBLANE
    cc = jnp.zeros((n_rows, d), F32).at[:bsz].set(c).at[bsz].set(c_ctx)
    mods = _ada_mods(cc, w_ada, b_ada)

    w_main = w_in[:, :, :n_main].astype(BF16)
    w_lr = jnp.zeros((depth, d, LANE), F32).at[:, :, :2 * GLA_RANK].set(w_in[:, :, n_main:]).astype(BF16)
    w2cat = jnp.zeros((depth, LANE, 2 * kd), F32)
    w2cat = w2cat.at[:, :GLA_RANK, :kd].set(gla_w2[:, 0]).at[:, GLA_RANK:2 * GLA_RANK, kd:].set(gla_w2[:, 1])
    w2cat = w2cat.astype(BF16)
    b2cat = gla_b2.reshape(depth, 1, 2 * kd)
    sgu_b_full = jnp.repeat(jnp.swapaxes(sgu_b, 1, 2), LANE, axis=2)

    zero_state = jnp.zeros((bsz, GLA_HEADS, hv, hk), F32)
    tm = 256
    for l in range(depth):
        last = l == depth - 1
        wts = {
            "norm1_g": norm1_g[l][None], "norm2_g": norm2_g[l][None],
            "w_main": w_main[l], "w_lr": w_lr[l], "w2cat": w2cat[l], "b2cat": b2cat[l],
            "sgu_ln_g": sgu_ln_g[l][None], "sgu_ln_b": sgu_ln_b[l][None],
            "sgu_w": sgu_w[l].astype(BF16), "sgu_b_full": sgu_b_full[l],
            "gla_norm_g": gla_norm_g[l][None],
            "w_br_a": w_br_a[l].astype(BF16), "w_br_b": w_br_b[l].astype(BF16), "w_o": w_o[l].astype(BF16),
            "ffn_w_up": ffn_w_up[l].astype(BF16), "ffn_conv_w": ffn_conv_w[l].reshape(CONV_K * CONV_K, dff),
            "ffn_conv_b": ffn_conv_b[l][None], "ffn_w_down": ffn_w_down[l].astype(BF16),
            "final_g": final_norm_g[None],
        }
        m = mods[l].reshape(n_rows, N_MOD, d)
        mods_x = [m[:bsz, j][:, None, :] for j in range(N_MOD)]
        mods_c = [jnp.broadcast_to(m[bsz, j][None, None, :], (bsz, 1, d)) for j in range(N_MOD)]
        ctx, s_f, s_b = _stream_layer(
            ctx, mods_c, wts, zero_state, zero_state, tm=min(tm, ctx_len), gla_tile=ctx_len, ffn_tile=ctx_len,
            img_w=ctx_len, states_only=last, final_norm=False)
        x, _, _ = _stream_layer(
            x, mods_x, wts, s_f, s_b, tm=tm, gla_tile=512, ffn_tile=512, img_w=GRID_W,
            states_only=False, final_norm=last)
    return x
```

```python
import functools

import jax
import jax.numpy as jnp
from jax import lax
from jax.experimental import pallas as pl
from jax.experimental.pallas import tpu as pltpu

F32 = jnp.float32
BF16 = jnp.bfloat16

EPS = 1e-6
N_MOD = 6
SGU_CHUNK = 128
SGU_GROUPS = 8
GLA_HEADS = 4
GLA_RANK = 16
GLA_GATE_NORMALIZER = 16.0
GLA_CHUNK = 64
GRID_W = 64
CONV_K = 3
LANE = 128
SUBLANE = 8
VMEM_LIMIT_BYTES = 56 * 1024 * 1024
INV_SQRT2 = 0.7071067811865476
FFN_DOWN_BLOCK = 128


def _params(*semantics):
    return pltpu.CompilerParams(dimension_semantics=semantics, vmem_limit_bytes=VMEM_LIMIT_BYTES)


def _const_spec(shape):
    zeros = (0,) * len(shape)
    return pl.BlockSpec(shape, lambda *_: zeros, pipeline_mode=pl.Buffered(1))


def _gelu(v):
    return 0.5 * v * (1.0 + lax.erf(v * INV_SQRT2))


def _sigmoid(v):
    return 1.0 / (1.0 + jnp.exp(-v))


def _log_sigmoid(v):
    return jnp.minimum(v, 0.0) - jnp.log1p(jnp.exp(-jnp.abs(v)))


def _rms_mod(x, g, shift, scale):
    ms = jnp.mean(x * x, axis=-1, keepdims=True)
    y = x * lax.rsqrt(ms + EPS) * g
    return y * (1.0 + scale) + shift


def _dot(a, b):
    return jnp.dot(a, b, preferred_element_type=F32)


def _dot_nt(a, b):
    return lax.dot_general(a, b, (((1,), (1,)), ((), ())), preferred_element_type=F32)


def _bf16_odd_tiles(w):
    tiles = -(-w.shape[-1] // LANE)
    tiles += 1 - tiles % 2
    pad = [(0, 0)] * (w.ndim - 1) + [(0, tiles * LANE - w.shape[-1])]
    return jnp.pad(w.astype(BF16), pad)


def _ada_kernel(c_ref, w_ref, b_ref, o_ref):
    c = c_ref[...]
    s = (c * _sigmoid(c)).astype(BF16)
    o_ref[0] = _dot(s, w_ref[0].astype(BF16)) + b_ref[0]


def _ada_mods(cc, w_ada, b_ada):
    depth, d, n = w_ada.shape
    tn = n // 4
    return pl.pallas_call(
        _ada_kernel,
        out_shape=jax.ShapeDtypeStruct((depth, cc.shape[0], n), F32),
        grid=(depth, n // tn),
        in_specs=[
            pl.BlockSpec(cc.shape, lambda l, j: (0, 0)),
            pl.BlockSpec((1, d, tn), lambda l, j: (l, 0, j)),
            pl.BlockSpec((1, 1, tn), lambda l, j: (l, 0, j)),
        ],
        out_specs=pl.BlockSpec((1, cc.shape[0], tn), lambda l, j: (l, 0, j)),
        compiler_params=_params("parallel", "parallel"),
        name="ada_mods",
    )(cc, w_ada, b_ada.reshape(depth, 1, n))


def _chunk_cumsum(x, reverse):
    n_sl = x.shape[0] // SUBLANE
    sub = lax.broadcasted_iota(jnp.int32, (SUBLANE, x.shape[1]), 0)
    tiles = []
    for j in range(n_sl):
        y = x[j * SUBLANE:(j + 1) * SUBLANE]
        for s in (1, 2, 4):
            if reverse:
                y = y + jnp.where(sub < SUBLANE - s, pltpu.roll(y, SUBLANE - s, 0), 0.0)
            else:
                y = y + jnp.where(sub >= s, pltpu.roll(y, s, 0), 0.0)
        tiles.append(y)
    edge = 0 if reverse else SUBLANE - 1
    carry = None
    for j in (range(n_sl - 1, -1, -1) if reverse else range(n_sl)):
        if carry is not None:
            tiles[j] = tiles[j] + carry
        carry = jnp.broadcast_to(tiles[j][edge:edge + 1], tiles[j].shape)
    return jnp.concatenate(tiles, axis=0), carry[0:1]


def _inproj_kernel(x_ref, sh_ref, sc_ref, g_ref, w_ref, w2_ref, b2_ref, lng_ref, lnb_ref,
                   gu_ref, vn_ref, v_ref, rs_ref, sga_ref, sgb_ref,
                   qdf_ref, kif_ref, kdtf_ref, qdb_ref, kib_ref, kdtb_ref, dec_ref,
                   *, d, kd, q_scale, tm):
    hb = _rms_mod(x_ref[0], g_ref[...], sh_ref[0], sc_ref[0]).astype(BF16)

    def seg(lo, width):
        return _dot(hb, w_ref[:, lo:lo + width])

    off_u, off_vs, off_q, off_r = 0, d, 2 * d, 2 * d + kd
    off_ga, off_gb, off_k, off_v = 3 * d + kd, 4 * d + kd, 5 * d + kd, 5 * d + 2 * kd
    off_lr = 6 * d + 2 * kd

    lr = seg(off_lr, LANE).astype(BF16)
    la = _log_sigmoid(_dot(lr, w2_ref[...]) + b2_ref[...]) * (1.0 / GLA_GATE_NORMALIZER)
    q = seg(off_q, kd) * q_scale
    k = seg(off_k, kd)

    for reverse, cols, qd_ref, ki_ref, kdt_ref in (
            (False, slice(0, kd), qdf_ref, kif_ref, kdtf_ref),
            (True, slice(kd, 2 * kd), qdb_ref, kib_ref, kdtb_ref)):
        k_dec = []
        for ci in range(tm // GLA_CHUNK):
            rows = slice(ci * GLA_CHUNK, (ci + 1) * GLA_CHUNK)
            b, b_last = _chunk_cumsum(la[rows, cols], reverse)
            qd_ref[0, rows, :] = (q[rows] * jnp.exp(b)).astype(BF16)
            ki_ref[0, rows, :] = (k[rows] * jnp.exp(-b)).astype(BF16)
            k_dec.append(k[rows] * jnp.exp(b_last - b))
            dec_ref[0, 0, ci:ci + 1, cols] = jnp.exp(b_last)
        kdt_ref[0] = jnp.concatenate(k_dec, axis=0).T.astype(BF16)

    gu_ref[0] = _gelu(seg(off_u, d)).astype(BF16)
    gv = _gelu(seg(off_vs, d))
    mu = jnp.mean(gv, axis=-1, keepdims=True)
    cen = gv - mu
    var = jnp.mean(cen * cen, axis=-1, keepdims=True)
    vn_ref[0] = (cen * lax.rsqrt(var + EPS) * lng_ref[...] + lnb_ref[...]).astype(BF16)
    r = seg(off_r, d)
    rs_ref[0] = (r * _sigmoid(r)).astype(BF16)
    sga_ref[0] = _sigmoid(seg(off_ga, d)).astype(BF16)
    sgb_ref[0] = _sigmoid(seg(off_gb, d)).astype(BF16)
    v_ref[0] = seg(off_v, d).astype(BF16)


def _in_proj(x, shift, scale, norm_g, w_all, w2cat, b2cat, ln_g, ln_b, *, tm):
    bsz, length, d = x.shape
    kd = d // 2
    hk = kd // GLA_HEADS
    row = lambda b, i: (b, i, 0)
    mod = lambda b, i: (b, 0, 0)
    wide = jax.ShapeDtypeStruct((bsz, length, d), BF16)
    half = jax.ShapeDtypeStruct((bsz, length, kd), BF16)
    halft = jax.ShapeDtypeStruct((bsz, kd, length), BF16)
    n_ch = tm // GLA_CHUNK
    dec = jax.ShapeDtypeStruct((bsz, length // tm, n_ch, 2 * kd), F32)
    wide_spec = pl.BlockSpec((1, tm, d), row)
    half_spec = pl.BlockSpec((1, tm, kd), row)
    halft_spec = pl.BlockSpec((1, kd, tm), lambda b, i: (b, 0, i))
    dec_spec = pl.BlockSpec((1, 1, n_ch, 2 * kd), lambda b, i: (b, i, 0, 0))
    return pl.pallas_call(
        functools.partial(_inproj_kernel, d=d, kd=kd, q_scale=float(hk) ** -0.5, tm=tm),
        out_shape=(wide,) * 6 + (half, half, halft, half, half, halft, dec),
        grid=(bsz, length // tm),
        in_specs=[
            pl.BlockSpec((1, tm, d), row),
            pl.BlockSpec((1, 1, d), mod),
            pl.BlockSpec((1, 1, d), mod),
            _const_spec((1, d)),
            _const_spec(w_all.shape),
            _const_spec(w2cat.shape),
            _const_spec(b2cat.shape),
            _const_spec((1, d)),
            _const_spec((1, d)),
        ],
        out_specs=(wide_spec,) * 6 + (half_spec, half_spec, halft_spec, half_spec, half_spec, halft_spec, dec_spec),
        compiler_params=_params("parallel", "parallel"),
        name="in_proj",
    )(x, shift, scale, norm_g, w_all, w2cat, b2cat, ln_g, ln_b)


def _gla_kernel(qdf_ref, kif_ref, kdtf_ref, vf_ref, decf_ref, qdb_ref, kib_ref, kdtb_ref, vb_ref, decb_ref,
                s0f_ref, s0b_ref, of_ref, ob_ref, sf_ref, sb_ref, *, n_chunks, hk, hv):
    @pl.when(pl.program_id(1) == 0)
    def _():
        sf_ref[...] = s0f_ref[...]
        sb_ref[...] = s0b_ref[...]

    r = lax.broadcasted_iota(jnp.int32, (GLA_CHUNK, GLA_CHUNK), 0)
    c = lax.broadcasted_iota(jnp.int32, (GLA_CHUNK, GLA_CHUNK), 1)
    lower = c <= r
    upper = c >= r

    def decay_columns(dec_ref):
        dec = dec_ref[0]
        if n_chunks < SUBLANE:
            dec = jnp.concatenate([dec, jnp.zeros((SUBLANE - n_chunks, dec.shape[1]), F32)], axis=0)
        return dec.T

    dect_f = decay_columns(decf_ref)
    dect_b = decay_columns(decb_ref)

    fwd = (qdf_ref, kif_ref, kdtf_ref, vf_ref, dect_f, of_ref, sf_ref, lower)
    bwd = (qdb_ref, kib_ref, kdtb_ref, vb_ref, dect_b, ob_ref, sb_ref, upper)
    for step in range(n_chunks):
        streams = [(refs, ci, h) for refs, ci in ((fwd, step), (bwd, n_chunks - 1 - step))
                   for h in range(GLA_HEADS)]
        raw, kvs = [], []
        for (qd_ref, ki_ref, kdt_ref, v_ref, _, _, _, _), ci, h in streams:
            rows = slice(ci * GLA_CHUNK, (ci + 1) * GLA_CHUNK)
            ks = slice(h * hk, (h + 1) * hk)
            raw.append(_dot_nt(qd_ref[0, rows, ks], ki_ref[0, rows, ks]))
            kvs.append(_dot(kdt_ref[0, ks, rows], v_ref[0, rows, h * hv:(h + 1) * hv]))
        for ((qd_ref, _, _, v_ref, dect, o_ref, st_ref, keep), ci, h), sc, kv in zip(streams, raw, kvs):
            rows = slice(ci * GLA_CHUNK, (ci + 1) * GLA_CHUNK)
            ks = slice(h * hk, (h + 1) * hk)
            vs = slice(h * hv, (h + 1) * hv)
            scores = jnp.where(keep, sc, 0.0).astype(BF16)
            state = st_ref[0, h]
            o_inter = _dot(qd_ref[0, rows, ks], state.astype(BF16))
            o_ref[0, rows, vs] = (_dot(scores, v_ref[0, rows, vs]) + o_inter).astype(BF16)
            st_ref[0, h] = state * jnp.broadcast_to(dect[ks, ci:ci + 1], (hk, hv)) + kv


def _gla(qd_f, ki_f, kdt_f, qd_b, ki_b, kdt_b, v, dec, s0f, s0b, *, tile):
    bsz, length, kd = qd_f.shape
    vd = v.shape[-1]
    hk, hv = kd // GLA_HEADS, vd // GLA_HEADS
    nt = length // tile
    n_chunks = tile // GLA_CHUNK
    fwd = lambda b, i: (b, i, 0)
    bwd = lambda b, i: (b, nt - 1 - i, 0)
    st_spec = pl.BlockSpec((1, GLA_HEADS, hk, hv), lambda b, i: (b, 0, 0, 0))
    out = jax.ShapeDtypeStruct((bsz, length, vd), BF16)

    def direction_specs(idx, dec_col):
        return [
            pl.BlockSpec((1, tile, kd), idx),
            pl.BlockSpec((1, tile, kd), idx),
            pl.BlockSpec((1, kd, tile), lambda b, i: (b, 0, idx(b, i)[1])),
            pl.BlockSpec((1, tile, vd), idx),
            pl.BlockSpec((1, n_chunks, kd), lambda b, i: (b, idx(b, i)[1], dec_col)),
        ]

    return pl.pallas_call(
        functools.partial(_gla_kernel, n_chunks=n_chunks, hk=hk, hv=hv),
        out_shape=(out, out, jax.ShapeDtypeStruct(s0f.shape, F32), jax.ShapeDtypeStruct(s0b.shape, F32)),
        grid=(bsz, nt),
        in_specs=direction_specs(fwd, 0) + direction_specs(bwd, 1) + [st_spec, st_spec],
        out_specs=(pl.BlockSpec((1, tile, vd), fwd), pl.BlockSpec((1, tile, vd), bwd), st_spec, st_spec),
        compiler_params=_params("parallel", "arbitrary"),
        name="gla_scan",
    )(qd_f, ki_f, kdt_f, v, dec, qd_b, ki_b, kdt_b, v, dec, s0f, s0b)


def _mix_kernel(x_ref, gu_ref, vn_ref, of_ref, ob_ref, rs_ref, sga_ref, sgb_ref, ws_ref, bs_ref, gn_ref,
                wa_ref, wb_ref, wo_ref, g1_ref, n2_ref, sh2_ref, sc2_ref, x1_ref, h2_ref, s_buf,
                *, tm, hv):
    for n in range(tm // SGU_CHUNK):
        rows = slice(n * SGU_CHUNK, (n + 1) * SGU_CHUNK)
        for g in range(SGU_GROUPS):
            cols = slice(g * LANE, (g + 1) * LANE)
            s_buf[rows, cols] = _dot(ws_ref[g], vn_ref[0, rows, cols]) + bs_ref[:, cols]
    a = (gu_ref[0].astype(F32) * s_buf[...]).astype(BF16)
    d = x_ref.shape[-1]
    proj_a = _dot(a, wa_ref[:, :d])
    o = of_ref[0].astype(F32) + ob_ref[0].astype(F32)
    for h in range(GLA_HEADS):
        cols = slice(h * hv, (h + 1) * hv)
        oh = o[:, cols]
        ms = jnp.mean(oh * oh, axis=-1, keepdims=True)
        s_buf[:, cols] = oh * lax.rsqrt(ms + EPS) * gn_ref[:, cols]
    ob = (s_buf[...] * rs_ref[0].astype(F32)).astype(BF16)
    proj_b = _dot(ob, wb_ref[:, :d])
    merged = (sga_ref[0].astype(F32) * proj_a + sgb_ref[0].astype(F32) * proj_b).astype(BF16)
    x1 = x_ref[0] + g1_ref[0] * _dot(merged, wo_ref[:, :d])
    x1_ref[0] = x1
    h2_ref[0] = _rms_mod(x1, n2_ref[...], sh2_ref[0], sc2_ref[0]).astype(BF16)


def _mix(x, gu, vn, o_f, o_b, rs, sga, sgb, ws, bs_full, gn_g, wa, wb, wo, g1, n2_g, sh2, sc2, *, tm):
    bsz, length, d = x.shape
    row = lambda b, i: (b, i, 0)
    mod = lambda b, i: (b, 0, 0)
    tile = pl.BlockSpec((1, tm, d), row)
    modv = pl.BlockSpec((1, 1, d), mod)
    return pl.pallas_call(
        functools.partial(_mix_kernel, tm=tm, hv=d // GLA_HEADS),
        out_shape=(jax.ShapeDtypeStruct((bsz, length, d), F32), jax.ShapeDtypeStruct((bsz, length, d), BF16)),
        grid=(bsz, length // tm),
        in_specs=[tile] * 8 + [
            _const_spec(ws.shape),
            _const_spec(bs_full.shape),
            _const_spec((1, d)),
            _const_spec(wa.shape),
            _const_spec(wb.shape),
            _const_spec(wo.shape),
            modv,
            _const_spec((1, d)),
            modv,
            modv,
        ],
        out_specs=(tile, tile),
        scratch_shapes=[pltpu.VMEM((tm, d), F32)],
        compiler_params=_params("parallel", "parallel"),
        name="mix",
    )(x, gu, vn, o_f, o_b, rs, sga, sgb, ws, bs_full, gn_g, wa, wb, wo, g1, n2_g, sh2, sc2)


def _ffn_up_kernel(h_ref, w_ref, a_ref, val_ref, *, dff):
    hb = h_ref[0]
    a_ref[0] = _dot(hb, w_ref[:, :dff]).astype(BF16)
    val_ref[0] = _dot(hb, w_ref[:, dff:]).astype(BF16)


def _ffn_up(h2, w_up, *, tm):
    bsz, length, d = h2.shape
    dff = w_up.shape[1] // 2
    row = lambda b, i: (b, i, 0)
    out = jax.ShapeDtypeStruct((bsz, length, dff), BF16)
    return pl.pallas_call(
        functools.partial(_ffn_up_kernel, dff=dff),
        out_shape=(out, out),
        grid=(bsz, length // tm),
        in_specs=[pl.BlockSpec((1, tm, d), row), _const_spec(w_up.shape)],
        out_specs=(pl.BlockSpec((1, tm, dff), row), pl.BlockSpec((1, tm, dff), row)),
        compiler_params=_params("parallel", "parallel"),
        name="ffn_up",
    )(h2, w_up)


def _ffn_down_kernel(*refs, tm, img_w, has_rows, final_norm):
    if has_rows:
        a_ref, prev_ref, next_ref = refs[:3]
        refs = refs[3:]
    else:
        a_ref = refs[0]
        refs = refs[1:]
    val_ref, cw_ref, cb_ref, wd_ref, x_ref, g2_ref, fg_ref, o_ref, pad = refs[:9]
    gated_refs = refs[9:]
    dff = a_ref.shape[-1]
    base = img_w if has_rows else 0
    i = pl.program_id(1)

    if has_rows:
        first = i == 0
        last = i == pl.num_programs(1) - 1
        pad[0:base, :] = jnp.where(first, 0.0, prev_ref[0].astype(F32))
        pad[base + tm:base + tm + img_w, :] = jnp.where(last, 0.0, next_ref[0].astype(F32))
    pad[base:base + tm, :] = a_ref[0].astype(F32)

    sub = lax.broadcasted_iota(jnp.int32, (SUBLANE, LANE), 0)
    not_first = sub >= 1
    not_last = sub <= SUBLANE - 2
    drs = (-1, 0, 1) if has_rows else (0,)
    blk = gated_refs[0].shape[0]

    def conv_row(r, gated, out_rows):
        tok = slice(r * img_w, (r + 1) * img_w)
        for s in range(dff // LANE):
            cols = slice(s * LANE, (s + 1) * LANE)
            acc = [None, None, None]
            for dr in drs:
                src = pad[base + (r + dr) * img_w:base + (r + dr + 1) * img_w, cols]
                for dc in (-1, 0, 1):
                    tap = cw_ref[(dr + 1) * CONV_K + dc + 1:(dr + 1) * CONV_K + dc + 2, cols]
                    acc[dc + 1] = tap * src if acc[dc + 1] is None else acc[dc + 1] + tap * src
            n_sl = img_w // SUBLANE
            down = [pltpu.roll(acc[0][j * SUBLANE:(j + 1) * SUBLANE], 1, 0) for j in range(n_sl)]
            up = [pltpu.roll(acc[2][j * SUBLANE:(j + 1) * SUBLANE], SUBLANE - 1, 0) for j in range(n_sl)]
            left = jnp.concatenate(
                [jnp.where(not_first, down[j], down[j - 1] if j > 0 else 0.0) for j in range(n_sl)], axis=0)
            right = jnp.concatenate(
                [jnp.where(not_last, up[j], up[j + 1] if j < n_sl - 1 else 0.0) for j in range(n_sl)], axis=0)
            conv = acc[1] + left + right + cb_ref[:, cols]
            gated[out_rows, cols] = (_gelu(conv) * val_ref[0, tok, cols].astype(F32)).astype(BF16)

    for j, gated in enumerate(gated_refs):
        for rr in range(blk // img_w):
            conv_row(j * (blk // img_w) + rr, gated, slice(rr * img_w, (rr + 1) * img_w))
        rows = slice(j * blk, (j + 1) * blk)
        x2 = x_ref[0, rows, :] + g2_ref[0] * _dot(gated[...], wd_ref[:, :x_ref.shape[-1]])
        if final_norm:
            ms = jnp.mean(x2 * x2, axis=-1, keepdims=True)
            x2 = x2 * lax.rsqrt(ms + EPS) * fg_ref[...]
        o_ref[0, rows, :] = x2


def _ffn_down(a, val, conv_w, conv_b, w_down, x1, g2, final_g, *, tm, img_w, final_norm):
    bsz, length, dff = a.shape
    d = x1.shape[-1]
    has_rows = length > img_w
    rows_per_tile = tm // img_w
    n_rows = length // img_w
    row = lambda b, i: (b, i, 0)
    mod = lambda b, i: (b, 0, 0)
    in_specs = [pl.BlockSpec((1, tm, dff), row)]
    args = [a]
    if has_rows:
        in_specs += [
            pl.BlockSpec((1, img_w, dff), lambda b, i: (b, jnp.maximum(i * rows_per_tile - 1, 0), 0)),
            pl.BlockSpec((1, img_w, dff), lambda b, i: (b, jnp.minimum((i + 1) * rows_per_tile, n_rows - 1), 0)),
        ]
        args += [a, a]
    in_specs += [
        pl.BlockSpec((1, tm, dff), row),
        _const_spec(conv_w.shape),
        _const_spec(conv_b.shape),
        _const_spec(w_down.shape),
        pl.BlockSpec((1, tm, d), row),
        pl.BlockSpec((1, 1, d), mod),
        _const_spec((1, d)),
    ]
    args += [val, conv_w, conv_b, w_down, x1, g2, final_g]
    pad_rows = tm + (2 * img_w if has_rows else 0)
    blk = max(FFN_DOWN_BLOCK, img_w)
    gated_scratch = [pltpu.VMEM((blk, dff), BF16) for _ in range(tm // blk)]
    return pl.pallas_call(
        functools.partial(_ffn_down_kernel, tm=tm, img_w=img_w, has_rows=has_rows, final_norm=final_norm),
        out_shape=jax.ShapeDtypeStruct(x1.shape, F32),
        grid=(bsz, length // tm),
        in_specs=in_specs,
        out_specs=pl.BlockSpec((1, tm, d), row),
        scratch_shapes=[pltpu.VMEM((pad_rows, dff), F32)] + gated_scratch,
        compiler_params=_params("parallel", "parallel"),
        name="ffn_down",
    )(*args)


def _stream_layer(x, mods, wts, s0f, s0b, *, tm, gla_tile, ffn_tile, img_w, states_only, final_norm):
    sh1, sc1, g1, sh2, sc2, g2 = mods
    bsz, length, d = x.shape
    gu, vn, v, rs, sga, sgb, qd_f, ki_f, kdt_f, qd_b, ki_b, kdt_b, dec = _in_proj(
        x, sh1, sc1, wts["norm1_g"], wts["w_all"], wts["w2cat"], wts["b2cat"],
        wts["sgu_ln_g"], wts["sgu_ln_b"], tm=tm)
    dec = dec.reshape(bsz, length // GLA_CHUNK, d)
    o_f, o_b, s_f, s_b = _gla(qd_f, ki_f, kdt_f, qd_b, ki_b, kdt_b, v, dec, s0f, s0b, tile=gla_tile)
    if states_only:
        return None, s_f, s_b
    x1, h2 = _mix(x, gu, vn, o_f, o_b, rs, sga, sgb, wts["sgu_w"], wts["sgu_b_full"], wts["gla_norm_g"],
                  wts["w_br_a"], wts["w_br_b"], wts["w_o"], g1, wts["norm2_g"], sh2, sc2, tm=tm)
    a, val = _ffn_up(h2, wts["ffn_w_up"], tm=tm)
    x2 = _ffn_down(a, val, wts["ffn_conv_w"], wts["ffn_conv_b"], wts["ffn_w_down"], x1, g2, wts["final_g"],
                   tm=ffn_tile, img_w=img_w, final_norm=final_norm)
    return x2, s_f, s_b


def kernel(x, c, ctx, c_ctx, w_ada, b_ada, norm1_g, norm2_g, w_in, sgu_ln_g, sgu_ln_b, sgu_w, sgu_b, gla_w2, gla_b2, gla_norm_g, w_br_a, w_br_b, w_o, ffn_w_up, ffn_conv_w, ffn_conv_b, ffn_w_down, final_norm_g):
    bsz, seq, d = x.shape
    ctx_len = ctx.shape[1]
    depth = w_in.shape[0]
    kd = d // 2
    hk, hv = kd // GLA_HEADS, d // GLA_HEADS
    dff = ffn_w_down.shape[1]

    n_rows = -(-(bsz + 1) // SUBLANE) * SUBLANE
    cc = jnp.zeros((n_rows, d), F32).at[:bsz].set(c).at[bsz].set(c_ctx)
    mods = _ada_mods(cc, w_ada, b_ada)

    w_all = _bf16_odd_tiles(w_in)
    w2cat = jnp.zeros((depth, LANE, 2 * kd), F32)
    w2cat = w2cat.at[:, :GLA_RANK, :kd].set(gla_w2[:, 0]).at[:, GLA_RANK:2 * GLA_RANK, kd:].set(gla_w2[:, 1])
    w2cat = w2cat.astype(BF16)
    b2cat = gla_b2.reshape(depth, 1, 2 * kd)
    sgu_b_full = jnp.repeat(jnp.swapaxes(sgu_b, 1, 2), LANE, axis=2)

    zero_state = jnp.zeros((bsz, GLA_HEADS, hk, hv), F32)
    tm = 256
    for l in range(depth):
        last = l == depth - 1
        wts = {
            "norm1_g": norm1_g[l][None], "norm2_g": norm2_g[l][None],
            "w_all": w_all[l], "w2cat": w2cat[l], "b2cat": b2cat[l],
            "sgu_ln_g": sgu_ln_g[l][None], "sgu_ln_b": sgu_ln_b[l][None],
            "sgu_w": sgu_w[l].astype(BF16), "sgu_b_full": sgu_b_full[l],
            "gla_norm_g": gla_norm_g[l][None],
            "w_br_a": _bf16_odd_tiles(w_br_a[l]), "w_br_b": _bf16_odd_tiles(w_br_b[l]),
            "w_o": _bf16_odd_tiles(w_o[l]),
            "ffn_w_up": ffn_w_up[l].astype(BF16), "ffn_conv_w": ffn_conv_w[l].reshape(CONV_K * CONV_K, dff),
            "ffn_conv_b": ffn_conv_b[l][None], "ffn_w_down": _bf16_odd_tiles(ffn_w_down[l]),
            "final_g": final_norm_g[None],
        }
        m = mods[l].reshape(n_rows, N_MOD, d)
        mods_x = [m[:bsz, j][:, None, :] for j in range(N_MOD)]
        mods_c = [jnp.broadcast_to(m[bsz, j][None, None, :], (bsz, 1, d)) for j in range(N_MOD)]
        ctx, s_f, s_b = _stream_layer(
            ctx, mods_c, wts, zero_state, zero_state, tm=min(tm, ctx_len), gla_tile=ctx_len, ffn_tile=ctx_len,
            img_w=ctx_len, states_only=last, final_norm=False)
        x, _, _ = _stream_layer(
            x, mods_x, wts, s_f, s_b, tm=tm, gla_tile=512, ffn_tile=512, img_w=GRID_W,
            states_only=False, final_norm=last)
    return x
```

```python
import functools

import jax
import jax.numpy as jnp
from jax import lax
from jax.experimental import pallas as pl
from jax.experimental.pallas import tpu as pltpu

F32 = jnp.float32
BF16 = jnp.bfloat16

EPS = 1e-6
N_MOD = 6
SGU_CHUNK = 128
SGU_GROUPS = 8
GLA_HEADS = 4
GLA_RANK = 16
GLA_GATE_NORMALIZER = 16.0
GLA_CHUNK = 64
GRID_W = 64
CONV_K = 3
LANE = 128
SUBLANE = 8
VMEM_LIMIT_BYTES = 56 * 1024 * 1024
INV_SQRT2 = 0.7071067811865476
FFN_DOWN_BLOCK = 128
UP_SLAB = 256


def _params(*semantics):
    return pltpu.CompilerParams(dimension_semantics=semantics, vmem_limit_bytes=VMEM_LIMIT_BYTES)


def _const_spec(shape):
    zeros = (0,) * len(shape)
    return pl.BlockSpec(shape, lambda *_: zeros, pipeline_mode=pl.Buffered(1))


def _gelu(v):
    return 0.5 * v * (1.0 + lax.erf(v * INV_SQRT2))


def _sigmoid(v):
    return 1.0 / (1.0 + jnp.exp(-v))


def _log_sigmoid(v):
    return jnp.minimum(v, 0.0) - jnp.log1p(jnp.exp(-jnp.abs(v)))


def _rms_mod(x, g, shift, scale):
    ms = jnp.mean(x * x, axis=-1, keepdims=True)
    y = x * lax.rsqrt(ms + EPS) * g
    return y * (1.0 + scale) + shift


def _dot(a, b):
    return jnp.dot(a, b, preferred_element_type=F32)


def _dot_nt(a, b):
    return lax.dot_general(a, b, (((1,), (1,)), ((), ())), preferred_element_type=F32)


def _bf16_odd_tiles(w):
    tiles = -(-w.shape[-1] // LANE)
    tiles += 1 - tiles % 2
    pad = [(0, 0)] * (w.ndim - 1) + [(0, tiles * LANE - w.shape[-1])]
    return jnp.pad(w.astype(BF16), pad)


def _ada_kernel(c_ref, w_ref, b_ref, o_ref):
    c = c_ref[...]
    s = (c * _sigmoid(c)).astype(BF16)
    o_ref[0] = _dot(s, w_ref[0].astype(BF16)) + b_ref[0]


def _ada_mods(cc, w_ada, b_ada):
    depth, d, n = w_ada.shape
    tn = n // 4
    return pl.pallas_call(
        _ada_kernel,
        out_shape=jax.ShapeDtypeStruct((depth, cc.shape[0], n), F32),
        grid=(depth, n // tn),
        in_specs=[
            pl.BlockSpec(cc.shape, lambda l, j: (0, 0)),
            pl.BlockSpec((1, d, tn), lambda l, j: (l, 0, j)),
            pl.BlockSpec((1, 1, tn), lambda l, j: (l, 0, j)),
        ],
        out_specs=pl.BlockSpec((1, cc.shape[0], tn), lambda l, j: (l, 0, j)),
        compiler_params=_params("parallel", "parallel"),
        name="ada_mods",
    )(cc, w_ada, b_ada.reshape(depth, 1, n))


def _chunk_cumsum(x, reverse):
    n_sl = x.shape[0] // SUBLANE
    sub = lax.broadcasted_iota(jnp.int32, (SUBLANE, x.shape[1]), 0)
    tiles = []
    for j in range(n_sl):
        y = x[j * SUBLANE:(j + 1) * SUBLANE]
        for s in (1, 2, 4):
            if reverse:
                y = y + jnp.where(sub < SUBLANE - s, pltpu.roll(y, SUBLANE - s, 0), 0.0)
            else:
                y = y + jnp.where(sub >= s, pltpu.roll(y, s, 0), 0.0)
        tiles.append(y)
    edge = 0 if reverse else SUBLANE - 1
    carry = None
    for j in (range(n_sl - 1, -1, -1) if reverse else range(n_sl)):
        if carry is not None:
            tiles[j] = tiles[j] + carry
        carry = jnp.broadcast_to(tiles[j][edge:edge + 1], tiles[j].shape)
    return jnp.concatenate(tiles, axis=0), carry[0:1]


def _inproj_kernel(x_ref, sh_ref, sc_ref, g_ref, w_ref, w2_ref, b2_ref, lng_ref, lnb_ref,
                   gu_ref, vn_ref, v_ref, rs_ref, sga_ref, sgb_ref,
                   qdf_ref, kif_ref, kdtf_ref, qdb_ref, kib_ref, kdtb_ref, dec_ref,
                   *, d, kd, q_scale, tm):
    hb = _rms_mod(x_ref[0], g_ref[...], sh_ref[0], sc_ref[0]).astype(BF16)

    def seg(lo, width):
        return _dot(hb, w_ref[:, lo:lo + width])

    off_u, off_vs, off_q, off_r = 0, d, 2 * d, 2 * d + kd
    off_ga, off_gb, off_k, off_v = 3 * d + kd, 4 * d + kd, 5 * d + kd, 5 * d + 2 * kd
    off_lr = 6 * d + 2 * kd

    lr = seg(off_lr, LANE).astype(BF16)
    la = _log_sigmoid(_dot(lr, w2_ref[...]) + b2_ref[...]) * (1.0 / GLA_GATE_NORMALIZER)
    q = seg(off_q, kd) * q_scale
    k = seg(off_k, kd)

    for reverse, cols, qd_ref, ki_ref, kdt_ref in (
            (False, slice(0, kd), qdf_ref, kif_ref, kdtf_ref),
            (True, slice(kd, 2 * kd), qdb_ref, kib_ref, kdtb_ref)):
        k_dec = []
        for ci in range(tm // GLA_CHUNK):
            rows = slice(ci * GLA_CHUNK, (ci + 1) * GLA_CHUNK)
            b, b_last = _chunk_cumsum(la[rows, cols], reverse)
            qd_ref[0, rows, :] = (q[rows] * jnp.exp(b)).astype(BF16)
            ki_ref[0, rows, :] = (k[rows] * jnp.exp(-b)).astype(BF16)
            k_dec.append(k[rows] * jnp.exp(b_last - b))
            dec_ref[0, 0, ci:ci + 1, cols] = jnp.exp(b_last)
        kdt_ref[0] = jnp.concatenate(k_dec, axis=0).T.astype(BF16)

    gu_ref[0] = _gelu(seg(off_u, d)).astype(BF16)
    gv = _gelu(seg(off_vs, d))
    mu = jnp.mean(gv, axis=-1, keepdims=True)
    cen = gv - mu
    var = jnp.mean(cen * cen, axis=-1, keepdims=True)
    vn_ref[0] = (cen * lax.rsqrt(var + EPS) * lng_ref[...] + lnb_ref[...]).astype(BF16)
    r = seg(off_r, d)
    rs_ref[0] = (r * _sigmoid(r)).astype(BF16)
    sga_ref[0] = _sigmoid(seg(off_ga, d)).astype(BF16)
    sgb_ref[0] = _sigmoid(seg(off_gb, d)).astype(BF16)
    v_ref[0] = seg(off_v, d).astype(BF16)


def _in_proj(x, shift, scale, norm_g, w_all, w2cat, b2cat, ln_g, ln_b, *, tm):
    bsz, length, d = x.shape
    kd = d // 2
    hk = kd // GLA_HEADS
    row = lambda b, i: (b, i, 0)
    mod = lambda b, i: (b, 0, 0)
    wide = jax.ShapeDtypeStruct((bsz, length, d), BF16)
    half = jax.ShapeDtypeStruct((bsz, length, kd), BF16)
    halft = jax.ShapeDtypeStruct((bsz, kd, length), BF16)
    n_ch = tm // GLA_CHUNK
    dec = jax.ShapeDtypeStruct((bsz, length // tm, n_ch, 2 * kd), F32)
    wide_spec = pl.BlockSpec((1, tm, d), row)
    half_spec = pl.BlockSpec((1, tm, kd), row)
    halft_spec = pl.BlockSpec((1, kd, tm), lambda b, i: (b, 0, i))
    dec_spec = pl.BlockSpec((1, 1, n_ch, 2 * kd), lambda b, i: (b, i, 0, 0))
    return pl.pallas_call(
        functools.partial(_inproj_kernel, d=d, kd=kd, q_scale=float(hk) ** -0.5, tm=tm),
        out_shape=(wide,) * 6 + (half, half, halft, half, half, halft, dec),
        grid=(bsz, length // tm),
        in_specs=[
            pl.BlockSpec((1, tm, d), row),
            pl.BlockSpec((1, 1, d), mod),
            pl.BlockSpec((1, 1, d), mod),
            _const_spec((1, d)),
            _const_spec(w_all.shape),
            _const_spec(w2cat.shape),
            _const_spec(b2cat.shape),
            _const_spec((1, d)),
            _const_spec((1, d)),
        ],
        out_specs=(wide_spec,) * 6 + (half_spec, half_spec, halft_spec, half_spec, half_spec, halft_spec, dec_spec),
        compiler_params=_params("parallel", "parallel"),
        name="in_proj",
    )(x, shift, scale, norm_g, w_all, w2cat, b2cat, ln_g, ln_b)


def _gla_kernel(qdf_ref, kif_ref, kdtf_ref, vf_ref, decf_ref, qdb_ref, kib_ref, kdtb_ref, vb_ref, decb_ref,
                s0f_ref, s0b_ref, of_ref, ob_ref, sf_ref, sb_ref, *, n_chunks, hk, hv):
    @pl.when(pl.program_id(1) == 0)
    def _():
        sf_ref[...] = s0f_ref[...]
        sb_ref[...] = s0b_ref[...]

    r = lax.broadcasted_iota(jnp.int32, (GLA_CHUNK, GLA_CHUNK), 0)
    c = lax.broadcasted_iota(jnp.int32, (GLA_CHUNK, GLA_CHUNK), 1)
    lower = c <= r
    upper = c >= r

    def decay_columns(dec_ref):
        dec = dec_ref[0]
        if n_chunks < SUBLANE:
            dec = jnp.concatenate([dec, jnp.zeros((SUBLANE - n_chunks, dec.shape[1]), F32)], axis=0)
        return dec.T

    dect_f = decay_columns(decf_ref)
    dect_b = decay_columns(decb_ref)

    fwd = (qdf_ref, kif_ref, kdtf_ref, vf_ref, dect_f, of_ref, sf_ref, lower)
    bwd = (qdb_ref, kib_ref, kdtb_ref, vb_ref, dect_b, ob_ref, sb_ref, upper)
    for step in range(n_chunks):
        streams = [(refs, ci, h) for refs, ci in ((fwd, step), (bwd, n_chunks - 1 - step))
                   for h in range(GLA_HEADS)]
        raw, kvs = [], []
        for (qd_ref, ki_ref, kdt_ref, v_ref, _, _, _, _), ci, h in streams:
            rows = slice(ci * GLA_CHUNK, (ci + 1) * GLA_CHUNK)
            ks = slice(h * hk, (h + 1) * hk)
            raw.append(_dot_nt(qd_ref[0, rows, ks], ki_ref[0, rows, ks]))
            kvs.append(_dot(kdt_ref[0, ks, rows], v_ref[0, rows, h * hv:(h + 1) * hv]))
        for ((qd_ref, _, _, v_ref, dect, o_ref, st_ref, keep), ci, h), sc, kv in zip(streams, raw, kvs):
            rows = slice(ci * GLA_CHUNK, (ci + 1) * GLA_CHUNK)
            ks = slice(h * hk, (h + 1) * hk)
            vs = slice(h * hv, (h + 1) * hv)
            scores = jnp.where(keep, sc, 0.0).astype(BF16)
            state = st_ref[0, h]
            o_inter = _dot(qd_ref[0, rows, ks], state.astype(BF16))
            o_ref[0, rows, vs] = (_dot(scores, v_ref[0, rows, vs]) + o_inter).astype(BF16)
            st_ref[0, h] = state * jnp.broadcast_to(dect[ks, ci:ci + 1], (hk, hv)) + kv


def _gla(qd_f, ki_f, kdt_f, qd_b, ki_b, kdt_b, v, dec, s0f, s0b, *, tile):
    bsz, length, kd = qd_f.shape
    vd = v.shape[-1]
    hk, hv = kd // GLA_HEADS, vd // GLA_HEADS
    nt = length // tile
    n_chunks = tile // GLA_CHUNK
    fwd = lambda b, i: (b, i, 0)
    bwd = lambda b, i: (b, nt - 1 - i, 0)
    st_spec = pl.BlockSpec((1, GLA_HEADS, hk, hv), lambda b, i: (b, 0, 0, 0))
    out = jax.ShapeDtypeStruct((bsz, length, vd), BF16)

    def direction_specs(idx, dec_col):
        return [
            pl.BlockSpec((1, tile, kd), idx),
            pl.BlockSpec((1, tile, kd), idx),
            pl.BlockSpec((1, kd, tile), lambda b, i: (b, 0, idx(b, i)[1])),
            pl.BlockSpec((1, tile, vd), idx),
            pl.BlockSpec((1, n_chunks, kd), lambda b, i: (b, idx(b, i)[1], dec_col)),
        ]

    return pl.pallas_call(
        functools.partial(_gla_kernel, n_chunks=n_chunks, hk=hk, hv=hv),
        out_shape=(out, out, jax.ShapeDtypeStruct(s0f.shape, F32), jax.ShapeDtypeStruct(s0b.shape, F32)),
        grid=(bsz, nt),
        in_specs=direction_specs(fwd, 0) + direction_specs(bwd, 1) + [st_spec, st_spec],
        out_specs=(pl.BlockSpec((1, tile, vd), fwd), pl.BlockSpec((1, tile, vd), bwd), st_spec, st_spec),
        compiler_params=_params("parallel", "arbitrary"),
        name="gla_scan",
    )(qd_f, ki_f, kdt_f, v, dec, qd_b, ki_b, kdt_b, v, dec, s0f, s0b)


def _mix_kernel(x_ref, gu_ref, vn_ref, of_ref, ob_ref, rs_ref, sga_ref, sgb_ref, ws_ref, bs_ref, gn_ref,
                wa_ref, wb_ref, wo_ref, g1_ref, n2_ref, sh2_ref, sc2_ref, x1_ref, h2_ref, s_buf,
                *, tm, hv):
    for n in range(tm // SGU_CHUNK):
        rows = slice(n * SGU_CHUNK, (n + 1) * SGU_CHUNK)
        for g in range(SGU_GROUPS):
            cols = slice(g * LANE, (g + 1) * LANE)
            s_buf[rows, cols] = _dot(ws_ref[g], vn_ref[0, rows, cols]) + bs_ref[:, cols]
    a = (gu_ref[0].astype(F32) * s_buf[...]).astype(BF16)
    d = x_ref.shape[-1]
    proj_a = _dot(a, wa_ref[:, :d])
    o = of_ref[0].astype(F32) + ob_ref[0].astype(F32)
    for h in range(GLA_HEADS):
        cols = slice(h * hv, (h + 1) * hv)
        oh = o[:, cols]
        ms = jnp.mean(oh * oh, axis=-1, keepdims=True)
        s_buf[:, cols] = oh * lax.rsqrt(ms + EPS) * gn_ref[:, cols]
    ob = (s_buf[...] * rs_ref[0].astype(F32)).astype(BF16)
    proj_b = _dot(ob, wb_ref[:, :d])
    merged = (sga_ref[0].astype(F32) * proj_a + sgb_ref[0].astype(F32) * proj_b).astype(BF16)
    x1 = x_ref[0] + g1_ref[0] * _dot(merged, wo_ref[:, :d])
    x1_ref[0] = x1
    h2_ref[0] = _rms_mod(x1, n2_ref[...], sh2_ref[0], sc2_ref[0]).astype(BF16)


def _mix(x, gu, vn, o_f, o_b, rs, sga, sgb, ws, bs_full, gn_g, wa, wb, wo, g1, n2_g, sh2, sc2, *, tm):
    bsz, length, d = x.shape
    row = lambda b, i: (b, i, 0)
    mod = lambda b, i: (b, 0, 0)
    tile = pl.BlockSpec((1, tm, d), row)
    modv = pl.BlockSpec((1, 1, d), mod)
    return pl.pallas_call(
        functools.partial(_mix_kernel, tm=tm, hv=d // GLA_HEADS),
        out_shape=(jax.ShapeDtypeStruct((bsz, length, d), F32), jax.ShapeDtypeStruct((bsz, length, d), BF16)),
        grid=(bsz, length // tm),
        in_specs=[tile] * 8 + [
            _const_spec(ws.shape),
            _const_spec(bs_full.shape),
            _const_spec((1, d)),
            _const_spec(wa.shape),
            _const_spec(wb.shape),
            _const_spec(wo.shape),
            modv,
            _const_spec((1, d)),
            modv,
            modv,
        ],
        out_specs=(tile, tile),
        scratch_shapes=[pltpu.VMEM((tm, d), F32)],
        compiler_params=_params("parallel", "parallel"),
        name="mix",
    )(x, gu, vn, o_f, o_b, rs, sga, sgb, ws, bs_full, gn_g, wa, wb, wo, g1, n2_g, sh2, sc2)


def _ffn_kernel(h_ref, x_ref, wu_ref, cw_ref, cb_ref, wd_ref, g2_ref, fg_ref, o_ref,
                a0_ref, a1_ref, val0_ref, val1_ref, above_ref, *gated_refs,
                n_tiles, tile, img_w, final_norm):
    s = pl.program_id(1)
    dff = a0_ref.shape[-1]
    d = x_ref.shape[-1]
    blk = gated_refs[0].shape[0]
    rows_img = tile // img_w
    has_rows = n_tiles * rows_img > 1
    bufs = ((a0_ref, val0_ref), (a1_ref, val1_ref))

    sub = lax.broadcasted_iota(jnp.int32, (SUBLANE, LANE), 0)
    not_first = sub >= 1
    not_last = sub <= SUBLANE - 2

    def up_pieces(a_dst, val_dst):
        def piece(rows, col, dst, cast):
            def run():
                res = _dot(h_ref[0, rows, :], wu_ref[:, col:col + UP_SLAB])
                dst[rows, col % dff:col % dff + UP_SLAB] = res.astype(BF16) if cast else res
            return run
        pieces = []
        for j in range(tile // blk):
            rows = slice(j * blk, (j + 1) * blk)
            pieces += [piece(rows, col, a_dst, False) for col in range(0, dff, UP_SLAB)]
            pieces += [piece(rows, dff + col, val_dst, True) for col in range(0, dff, UP_SLAB)]
        return pieces

    def conv_strip(cur, val, below, r, c, gated, out_rows):
        tok = slice(r * img_w, (r + 1) * img_w)
        cols = slice(c * LANE, (c + 1) * LANE)
        srcs = [(1, cur, tok)]
        if has_rows:
            srcs.append((0, cur, slice((r - 1) * img_w, r * img_w)) if r > 0 else (0, above_ref, slice(0, img_w)))
            if r < rows_img - 1:
                srcs.append((2, cur, slice((r + 1) * img_w, (r + 2) * img_w)))
            elif below is not None:
                srcs.append((2, below, slice(0, img_w)))
        acc = [None, None, None]
        for kr, ref, rws in srcs:
            src = ref[rws, cols]
            for dc in (-1, 0, 1):
                tap = cw_ref[kr * CONV_K + dc + 1:kr * CONV_K + dc + 2, cols]
                acc[dc + 1] = tap * src if acc[dc + 1] is None else acc[dc + 1] + tap * src
        n_sl = img_w // SUBLANE
        dn = [pltpu.roll(acc[0][j * SUBLANE:(j + 1) * SUBLANE], 1, 0) for j in range(n_sl)]
        upw = [pltpu.roll(acc[2][j * SUBLANE:(j + 1) * SUBLANE], SUBLANE - 1, 0) for j in range(n_sl)]
        left = jnp.concatenate(
            [jnp.where(not_first, dn[j], dn[j - 1] if j > 0 else 0.0) for j in range(n_sl)], axis=0)
        right = jnp.concatenate(
            [jnp.where(not_last, upw[j], upw[j + 1] if j < n_sl - 1 else 0.0) for j in range(n_sl)], axis=0)
        conv = acc[1] + left + right + cb_ref[:, cols]
        gated[out_rows, cols] = (_gelu(conv) * val[tok, cols].astype(F32)).astype(BF16)

    def conv_down(cur, val, below, pieces=()):
        slabs = dff // UP_SLAB
        total = len(gated_refs) * slabs
        done = issued = 0
        for j, gated in enumerate(gated_refs):
            y = None
            for k in range(slabs):
                for rr in range(blk // img_w):
                    for c in range(k * (UP_SLAB // LANE), (k + 1) * (UP_SLAB // LANE)):
                        conv_strip(cur, val, below, j * (blk // img_w) + rr, c, gated,
                                   slice(rr * img_w, (rr + 1) * img_w))
                done += 1
                while issued * total < done * len(pieces):
                    pieces[issued]()
                    issued += 1
                ks = slice(k * UP_SLAB, (k + 1) * UP_SLAB)
                part = _dot(gated[:, ks], wd_ref[ks, :d])
                y = part if y is None else y + part
            rows = slice(j * blk, (j + 1) * blk)
            x2 = x_ref[0, rows, :] + g2_ref[0] * y
            if final_norm:
                ms = jnp.mean(x2 * x2, axis=-1, keepdims=True)
                x2 = x2 * lax.rsqrt(ms + EPS) * fg_ref[...]
            o_ref[0, rows, :] = x2
        if has_rows:
            above_ref[...] = cur[tile - img_w:tile, :]

    @pl.when(s == 0)
    def _():
        if has_rows:
            above_ref[...] = jnp.zeros(above_ref.shape, F32)
        for run in up_pieces(*bufs[0]):
            run()

    for parity in sorted({step % 2 for step in range(1, n_tiles)}):
        @pl.when(jnp.logical_and(jnp.logical_and(s > 0, s < n_tiles), lax.rem(s, 2) == parity))
        def _():
            new, (cur, val) = bufs[parity], bufs[1 - parity]
            conv_down(cur, val, new[0], up_pieces(*new))

    @pl.when(s == n_tiles)
    def _():
        conv_down(*bufs[(n_tiles - 1) % 2], None)


def _ffn(h2, x1, w_up, conv_w, conv_b, w_down, g2, final_g, *, tile, img_w, final_norm):
    bsz, length, d = x1.shape
    dff = w_down.shape[0]
    n_tiles = length // tile
    assert n_tiles > 1 or tile == img_w, "a single tile must be a single image row"
    blk = max(FFN_DOWN_BLOCK, img_w)
    up_tile = lambda b, s: (b, jnp.minimum(s, n_tiles - 1), 0)
    down_tile = lambda b, s: (b, jnp.maximum(s - 1, 0), 0)
    return pl.pallas_call(
        functools.partial(_ffn_kernel, n_tiles=n_tiles, tile=tile, img_w=img_w, final_norm=final_norm),
        out_shape=jax.ShapeDtypeStruct(x1.shape, F32),
        grid=(bsz, n_tiles + 1),
        in_specs=[
            pl.BlockSpec((1, tile, d), up_tile),
            pl.BlockSpec((1, tile, d), down_tile),
            _const_spec(w_up.shape),
            _const_spec(conv_w.shape),
            _const_spec(conv_b.shape),
            _const_spec(w_down.shape),
            pl.BlockSpec((1, 1, d), lambda b, s: (b, 0, 0)),
            _const_spec((1, d)),
        ],
        out_specs=pl.BlockSpec((1, tile, d), down_tile),
        scratch_shapes=[pltpu.VMEM((tile, dff), F32), pltpu.VMEM((tile, dff), F32),
                        pltpu.VMEM((tile, dff), BF16), pltpu.VMEM((tile, dff), BF16),
                        pltpu.VMEM((img_w, dff), F32)]
        + [pltpu.VMEM((blk, dff), BF16) for _ in range(tile // blk)],
        compiler_params=_params("parallel", "arbitrary"),
        name="ffn",
    )(h2, x1, w_up, conv_w, conv_b, w_down, g2, final_g)


def _stream_layer(x, mods, wts, s0f, s0b, *, tm, gla_tile, ffn_tile, img_w, states_only, final_norm):
    sh1, sc1, g1, sh2, sc2, g2 = mods
    bsz, length, d = x.shape
    gu, vn, v, rs, sga, sgb, qd_f, ki_f, kdt_f, qd_b, ki_b, kdt_b, dec = _in_proj(
        x, sh1, sc1, wts["norm1_g"], wts["w_all"], wts["w2cat"], wts["b2cat"],
        wts["sgu_ln_g"], wts["sgu_ln_b"], tm=tm)
    dec = dec.reshape(bsz, length // GLA_CHUNK, d)
    o_f, o_b, s_f, s_b = _gla(qd_f, ki_f, kdt_f, qd_b, ki_b, kdt_b, v, dec, s0f, s0b, tile=gla_tile)
    if states_only:
        return None, s_f, s_b
    x1, h2 = _mix(x, gu, vn, o_f, o_b, rs, sga, sgb, wts["sgu_w"], wts["sgu_b_full"], wts["gla_norm_g"],
                  wts["w_br_a"], wts["w_br_b"], wts["w_o"], g1, wts["norm2_g"], sh2, sc2, tm=tm)
    x2 = _ffn(h2, x1, wts["ffn_w_up"], wts["ffn_conv_w"], wts["ffn_conv_b"], wts["ffn_w_down"], g2, wts["final_g"],
              tile=ffn_tile, img_w=img_w, final_norm=final_norm)
    return x2, s_f, s_b


def kernel(x, c, ctx, c_ctx, w_ada, b_ada, norm1_g, norm2_g, w_in, sgu_ln_g, sgu_ln_b, sgu_w, sgu_b, gla_w2, gla_b2, gla_norm_g, w_br_a, w_br_b, w_o, ffn_w_up, ffn_conv_w, ffn_conv_b, ffn_w_down, final_norm_g):
    bsz, seq, d = x.shape
    ctx_len = ctx.shape[1]
    depth = w_in.shape[0]
    kd = d // 2
    hk, hv = kd // GLA_HEADS, d // GLA_HEADS
    dff = ffn_w_down.shape[1]

    n_rows = -(-(bsz + 1) // SUBLANE) * SUBLANE
    cc = jnp.zeros((n_rows, d), F32).at[:bsz].set(c).at[bsz].set(c_ctx)
    mods = _ada_mods(cc, w_ada, b_ada)

    w_all = _bf16_odd_tiles(w_in)
    w2cat = jnp.zeros((depth, LANE, 2 * kd), F32)
    w2cat = w2cat.at[:, :GLA_RANK, :kd].set(gla_w2[:, 0]).at[:, GLA_RANK:2 * GLA_RANK, kd:].set(gla_w2[:, 1])
    w2cat = w2cat.astype(BF16)
    b2cat = gla_b2.reshape(depth, 1, 2 * kd)
    sgu_b_full = jnp.repeat(jnp.swapaxes(sgu_b, 1, 2), LANE, axis=2)

    zero_state = jnp.zeros((bsz, GLA_HEADS, hk, hv), F32)
    tm = 256
    for l in range(depth):
        last = l == depth - 1
        wts = {
            "norm1_g": norm1_g[l][None], "norm2_g": norm2_g[l][None],
            "w_all": w_all[l], "w2cat": w2cat[l], "b2cat": b2cat[l],
            "sgu_ln_g": sgu_ln_g[l][None], "sgu_ln_b": sgu_ln_b[l][None],
            "sgu_w": sgu_w[l].astype(BF16), "sgu_b_full": sgu_b_full[l],
            "gla_norm_g": gla_norm_g[l][None],
            "w_br_a": _bf16_odd_tiles(w_br_a[l]), "w_br_b": _bf16_odd_tiles(w_br_b[l]),
            "w_o": _bf16_odd_tiles(w_o[l]),
            "ffn_w_up": ffn_w_up[l].astype(BF16), "ffn_conv_w": ffn_conv_w[l].reshape(CONV_K * CONV_K, dff),
            "ffn_conv_b": ffn_conv_b[l][None], "ffn_w_down": _bf16_odd_tiles(ffn_w_down[l]),
            "final_g": final_norm_g[None],
        }
        m = mods[l].reshape(n_rows, N_MOD, d)
        mods_x = [m[:bsz, j][:, None, :] for j in range(N_MOD)]
        mods_c = [jnp.broadcast_to(m[bsz, j][None, None, :], (bsz, 1, d)) for j in range(N_MOD)]
        ctx, s_f, s_b = _stream_layer(
            ctx, mods_c, wts, zero_state, zero_state, tm=min(tm, ctx_len), gla_tile=ctx_len, ffn_tile=ctx_len,
            img_w=ctx_len, states_only=last, final_norm=False)
        x, _, _ = _stream_layer(
            x, mods_x, wts, s_f, s_b, tm=tm, gla_tile=512, ffn_tile=256, img_w=GRID_W,
            states_only=False, final_norm=last)
    return x
```

```python
import functools

import jax
import jax.numpy as jnp
from jax import lax
from jax.experimental import pallas as pl
from jax.experimental.pallas import tpu as pltpu

F32 = jnp.float32
BF16 = jnp.bfloat16

EPS = 1e-6
N_MOD = 6
SGU_CHUNK = 128
SGU_GROUPS = 8
GLA_HEADS = 4
GLA_RANK = 16
GLA_GATE_NORMALIZER = 16.0
GLA_CHUNK = 64
GRID_W = 64
CONV_K = 3
LANE = 128
SUBLANE = 8
VMEM_LIMIT_BYTES = 56 * 1024 * 1024
INV_SQRT2 = 0.7071067811865476
FFN_DOWN_BLOCK = 128
UP_SLAB = 256


def _params(*semantics):
    return pltpu.CompilerParams(dimension_semantics=semantics, vmem_limit_bytes=VMEM_LIMIT_BYTES)


def _const_spec(shape):
    zeros = (0,) * len(shape)
    return pl.BlockSpec(shape, lambda *_: zeros, pipeline_mode=pl.Buffered(1))


def _gelu(v):
    return 0.5 * v * (1.0 + lax.erf(v * INV_SQRT2))


def _sigmoid(v):
    return 1.0 / (1.0 + jnp.exp(-v))


def _log_sigmoid(v):
    return jnp.minimum(v, 0.0) - jnp.log(1.0 + jnp.exp(-jnp.abs(v)))


def _rms_mod(x, g, shift, scale):
    ms = jnp.mean(x * x, axis=-1, keepdims=True)
    y = x * lax.rsqrt(ms + EPS) * g
    return y * (1.0 + scale) + shift


def _dot(a, b):
    return jnp.dot(a, b, preferred_element_type=F32)


def _dot_nt(a, b):
    return lax.dot_general(a, b, (((1,), (1,)), ((), ())), preferred_element_type=F32)


def _bf16_odd_tiles(w):
    tiles = -(-w.shape[-1] // LANE)
    tiles += 1 - tiles % 2
    pad = [(0, 0)] * (w.ndim - 1) + [(0, tiles * LANE - w.shape[-1])]
    return jnp.pad(w.astype(BF16), pad)


def _ada_kernel(c_ref, w_ref, b_ref, o_ref):
    c = c_ref[...]
    s = (c * _sigmoid(c)).astype(BF16)
    o_ref[0] = _dot(s, w_ref[0].astype(BF16)) + b_ref[0]


def _ada_mods(cc, w_ada, b_ada):
    depth, d, n = w_ada.shape
    tn = n // 4
    return pl.pallas_call(
        _ada_kernel,
        out_shape=jax.ShapeDtypeStruct((depth, cc.shape[0], n), F32),
        grid=(depth, n // tn),
        in_specs=[
            pl.BlockSpec(cc.shape, lambda l, j: (0, 0)),
            pl.BlockSpec((1, d, tn), lambda l, j: (l, 0, j)),
            pl.BlockSpec((1, 1, tn), lambda l, j: (l, 0, j)),
        ],
        out_specs=pl.BlockSpec((1, cc.shape[0], tn), lambda l, j: (l, 0, j)),
        compiler_params=_params("parallel", "parallel"),
        name="ada_mods",
    )(cc, w_ada, b_ada.reshape(depth, 1, n))


def _chunk_cumsum(x, reverse):
    n_sl = x.shape[0] // SUBLANE
    sub = lax.broadcasted_iota(jnp.int32, (SUBLANE, x.shape[1]), 0)
    tiles = []
    for j in range(n_sl):
        y = x[j * SUBLANE:(j + 1) * SUBLANE]
        for s in (1, 2, 4):
            if reverse:
                y = y + jnp.where(sub < SUBLANE - s, pltpu.roll(y, SUBLANE - s, 0), 0.0)
            else:
                y = y + jnp.where(sub >= s, pltpu.roll(y, s, 0), 0.0)
        tiles.append(y)
    edge = 0 if reverse else SUBLANE - 1
    carry = None
    for j in (range(n_sl - 1, -1, -1) if reverse else range(n_sl)):
        if carry is not None:
            tiles[j] = tiles[j] + carry
        carry = jnp.broadcast_to(tiles[j][edge:edge + 1], tiles[j].shape)
    return jnp.concatenate(tiles, axis=0)


def _inproj_kernel(x_ref, sh_ref, sc_ref, g_ref, w_ref, w2_ref, b2_ref, lng_ref, lnb_ref,
                   gu_ref, vn_ref, v_ref, rs_ref, sga_ref, sgb_ref, q_ref, k_ref, la_ref,
                   *, d, kd, q_scale):
    hb = _rms_mod(x_ref[0], g_ref[...], sh_ref[0], sc_ref[0]).astype(BF16)

    def seg(lo, width):
        return _dot(hb, w_ref[:, lo:lo + width])

    off_u, off_vs, off_q, off_r = 0, d, 2 * d, 2 * d + kd
    off_ga, off_gb, off_k, off_v = 3 * d + kd, 4 * d + kd, 5 * d + kd, 5 * d + 2 * kd
    off_lr = 6 * d + 2 * kd

    lr = seg(off_lr, LANE)
    q = seg(off_q, kd)
    lr = lr.astype(BF16)
    k = seg(off_k, kd)
    q_ref[0] = q * q_scale
    z = _dot(lr, w2_ref[...]) + b2_ref[...]
    k_ref[0] = k
    ga = seg(off_ga, d)
    la = _log_sigmoid(z) * (1.0 / GLA_GATE_NORMALIZER)
    for ci in range(x_ref.shape[1] // GLA_CHUNK):
        rows = slice(ci * GLA_CHUNK, (ci + 1) * GLA_CHUNK)
        la_ref[0, rows, :kd] = _chunk_cumsum(la[rows, :kd], False)
        la_ref[0, rows, kd:] = _chunk_cumsum(la[rows, kd:], True)
    gb = seg(off_gb, d)
    sga_ref[0] = _sigmoid(ga).astype(BF16)
    r = seg(off_r, d)
    sgb_ref[0] = _sigmoid(gb).astype(BF16)
    u = seg(off_u, d)
    rs_ref[0] = (r * _sigmoid(r)).astype(BF16)
    vs = seg(off_vs, d)
    gu_ref[0] = _gelu(u).astype(BF16)
    v = seg(off_v, d)
    gv = _gelu(vs)
    mu = jnp.mean(gv, axis=-1, keepdims=True)
    cen = gv - mu
    var = jnp.mean(cen * cen, axis=-1, keepdims=True)
    vn_ref[0] = (cen * lax.rsqrt(var + EPS) * lng_ref[...] + lnb_ref[...]).astype(BF16)
    v_ref[0] = v.astype(BF16)


def _in_proj(x, shift, scale, norm_g, w_all, w2cat, b2cat, ln_g, ln_b, *, tm):
    bsz, length, d = x.shape
    kd = d // 2
    hk = kd // GLA_HEADS
    row = lambda b, i: (b, i, 0)
    mod = lambda b, i: (b, 0, 0)
    wide = jax.ShapeDtypeStruct((bsz, length, d), BF16)
    half = jax.ShapeDtypeStruct((bsz, length, kd), F32)
    wide_spec = pl.BlockSpec((1, tm, d), row)
    half_spec = pl.BlockSpec((1, tm, kd), row)
    return pl.pallas_call(
        functools.partial(_inproj_kernel, d=d, kd=kd, q_scale=float(hk) ** -0.5),
        out_shape=(wide,) * 6 + (half, half, jax.ShapeDtypeStruct((bsz, length, 2 * kd), F32)),
        grid=(bsz, length // tm),
        in_specs=[
            pl.BlockSpec((1, tm, d), row),
            pl.BlockSpec((1, 1, d), mod),
            pl.BlockSpec((1, 1, d), mod),
            _const_spec((1, d)),
            _const_spec(w_all.shape),
            _const_spec(w2cat.shape),
            _const_spec(b2cat.shape),
            _const_spec((1, d)),
            _const_spec((1, d)),
        ],
        out_specs=(wide_spec,) * 6 + (half_spec, half_spec, pl.BlockSpec((1, tm, 2 * kd), row)),
        compiler_params=_params("parallel", "parallel"),
        name="in_proj",
    )(x, shift, scale, norm_g, w_all, w2cat, b2cat, ln_g, ln_b)


def _gla_kernel(qf_ref, kf_ref, laf_ref, vf_ref, qb_ref, kb_ref, lab_ref, vb_ref, s0f_ref, s0b_ref,
                of_ref, ob_ref, sf_ref, sb_ref, *, n_chunks, hk, hv):
    @pl.when(pl.program_id(1) == 0)
    def _():
        sf_ref[...] = s0f_ref[...]
        sb_ref[...] = s0b_ref[...]

    r = lax.broadcasted_iota(jnp.int32, (GLA_CHUNK, GLA_CHUNK), 0)
    c = lax.broadcasted_iota(jnp.int32, (GLA_CHUNK, GLA_CHUNK), 1)
    fwd = (qf_ref, kf_ref, laf_ref, vf_ref, of_ref, sf_ref, c <= r, False)
    bwd = (qb_ref, kb_ref, lab_ref, vb_ref, ob_ref, sb_ref, c >= r, True)

    def decayed_operands(refs, ci):
        q_ref, k_ref, b_ref, _, _, _, _, reverse = refs
        rows = slice(ci * GLA_CHUNK, (ci + 1) * GLA_CHUNK)
        b = b_ref[0, rows, :]
        b_last = b[0:1] if reverse else b[GLA_CHUNK - 1:GLA_CHUNK]
        k = k_ref[0, rows, :]
        qd = (q_ref[0, rows, :] * jnp.exp(b)).astype(BF16)
        ki = (k * jnp.exp(-b)).astype(BF16)
        kdt = (k * jnp.exp(b_last - b)).T.astype(BF16)
        dec = jnp.broadcast_to(jnp.exp(b_last), (SUBLANE, b.shape[1])).T
        return qd, ki, kdt, dec

    def chunks_of(step):
        return ((fwd, step), (bwd, n_chunks - 1 - step))

    ops = [decayed_operands(refs, ci) for refs, ci in chunks_of(0)]
    for step in range(n_chunks):
        streams = [(refs, ci, h, op) for (refs, ci), op in zip(chunks_of(step), ops) for h in range(GLA_HEADS)]
        raw, kvs = [], []
        for (_, _, _, v_ref, _, _, _, _), ci, h, (qd, ki, kdt, _) in streams:
            rows = slice(ci * GLA_CHUNK, (ci + 1) * GLA_CHUNK)
            ks = slice(h * hk, (h + 1) * hk)
            raw.append(_dot_nt(qd[:, ks], ki[:, ks]))
            kvs.append(_dot(kdt[ks, :], v_ref[0, rows, h * hv:(h + 1) * hv]))
        if step + 1 < n_chunks:
            ops = [decayed_operands(refs, ci) for refs, ci in chunks_of(step + 1)]
        for ((_, _, _, v_ref, o_ref, st_ref, keep, _), ci, h, (qd, _, _, dec)), sc, kv in zip(streams, raw, kvs):
            rows = slice(ci * GLA_CHUNK, (ci + 1) * GLA_CHUNK)
            ks = slice(h * hk, (h + 1) * hk)
            vs = slice(h * hv, (h + 1) * hv)
            scores = jnp.where(keep, sc, 0.0).astype(BF16)
            state = st_ref[0, h]
            o_inter = _dot(qd[:, ks], state.astype(BF16))
            o_ref[0, rows, vs] = (_dot(scores, v_ref[0, rows, vs]) + o_inter).astype(BF16)
            st_ref[0, h] = state * jnp.broadcast_to(dec[ks, 0:1], (hk, hv)) + kv


def _gla(q, k, v, la, s0f, s0b, *, tile):
    bsz, length, kd = q.shape
    vd = v.shape[-1]
    hk, hv = kd // GLA_HEADS, vd // GLA_HEADS
    nt = length // tile
    fwd = lambda b, i: (b, i, 0)
    bwd = lambda b, i: (b, nt - 1 - i, 0)
    st_spec = pl.BlockSpec((1, GLA_HEADS, hk, hv), lambda b, i: (b, 0, 0, 0))
    out = jax.ShapeDtypeStruct((bsz, length, vd), BF16)

    def direction_specs(idx, la_col):
        return [
            pl.BlockSpec((1, tile, kd), idx),
            pl.BlockSpec((1, tile, kd), idx),
            pl.BlockSpec((1, tile, kd), lambda b, i: (b, idx(b, i)[1], la_col)),
            pl.BlockSpec((1, tile, vd), idx),
        ]

    return pl.pallas_call(
        functools.partial(_gla_kernel, n_chunks=tile // GLA_CHUNK, hk=hk, hv=hv),
        out_shape=(out, out, jax.ShapeDtypeStruct(s0f.shape, F32), jax.ShapeDtypeStruct(s0b.shape, F32)),
        grid=(bsz, nt),
        in_specs=direction_specs(fwd, 0) + direction_specs(bwd, 1) + [st_spec, st_spec],
        out_specs=(pl.BlockSpec((1, tile, vd), fwd), pl.BlockSpec((1, tile, vd), bwd), st_spec, st_spec),
        compiler_params=_params("parallel", "arbitrary"),
        name="gla_scan",
    )(q, k, la, v, q, k, la, v, s0f, s0b)


def _mix_kernel(x_ref, gu_ref, vn_ref, of_ref, ob_ref, rs_ref, sga_ref, sgb_ref, ws_ref, bs_ref, gn_ref,
                wa_ref, wb_ref, wo_ref, g1_ref, n2_ref, sh2_ref, sc2_ref, x1_ref, h2_ref, s_buf,
                *, tm, hv):
    for n in range(tm // SGU_CHUNK):
        rows = slice(n * SGU_CHUNK, (n + 1) * SGU_CHUNK)
        for g in range(SGU_GROUPS):
            cols = slice(g * LANE, (g + 1) * LANE)
            s_buf[rows, cols] = _dot(ws_ref[g], vn_ref[0, rows, cols]) + bs_ref[:, cols]
    a = (gu_ref[0].astype(F32) * s_buf[...]).astype(BF16)
    d = x_ref.shape[-1]
    proj_a = _dot(a, wa_ref[:, :d])
    o = of_ref[0].astype(F32) + ob_ref[0].astype(F32)
    for h in range(GLA_HEADS):
        cols = slice(h * hv, (h + 1) * hv)
        oh = o[:, cols]
        ms = jnp.mean(oh * oh, axis=-1, keepdims=True)
        s_buf[:, cols] = oh * lax.rsqrt(ms + EPS) * gn_ref[:, cols]
    ob = (s_buf[...] * rs_ref[0].astype(F32)).astype(BF16)
    proj_b = _dot(ob, wb_ref[:, :d])
    merged = (sga_ref[0].astype(F32) * proj_a + sgb_ref[0].astype(F32) * proj_b).astype(BF16)
    x1 = x_ref[0] + g1_ref[0] * _dot(merged, wo_ref[:, :d])
    x1_ref[0] = x1
    h2_ref[0] = _rms_mod(x1, n2_ref[...], sh2_ref[0], sc2_ref[0]).astype(BF16)


def _mix(x, gu, vn, o_f, o_b, rs, sga, sgb, ws, bs_full, gn_g, wa, wb, wo, g1, n2_g, sh2, sc2, *, tm):
    bsz, length, d = x.shape
    row = lambda b, i: (b, i, 0)
    mod = lambda b, i: (b, 0, 0)
    tile = pl.BlockSpec((1, tm, d), row)
    modv = pl.BlockSpec((1, 1, d), mod)
    return pl.pallas_call(
        functools.partial(_mix_kernel, tm=tm, hv=d // GLA_HEADS),
        out_shape=(jax.ShapeDtypeStruct((bsz, length, d), F32), jax.ShapeDtypeStruct((bsz, length, d), BF16)),
        grid=(bsz, length // tm),
        in_specs=[tile] * 8 + [
            _const_spec(ws.shape),
            _const_spec(bs_full.shape),
            _const_spec((1, d)),
            _const_spec(wa.shape),
            _const_spec(wb.shape),
            _const_spec(wo.shape),
            modv,
            _const_spec((1, d)),
            modv,
            modv,
        ],
        out_specs=(tile, tile),
        scratch_shapes=[pltpu.VMEM((tm, d), F32)],
        compiler_params=_params("parallel", "parallel"),
        name="mix",
    )(x, gu, vn, o_f, o_b, rs, sga, sgb, ws, bs_full, gn_g, wa, wb, wo, g1, n2_g, sh2, sc2)


def _ffn_kernel(h_ref, x_ref, wu_ref, cw_ref, cb_ref, wd_ref, g2_ref, fg_ref, o_ref,
                a0_ref, a1_ref, val0_ref, val1_ref, above_ref, *gated_refs,
                n_tiles, tile, img_w, final_norm):
    s = pl.program_id(1)
    dff = a0_ref.shape[-1]
    d = x_ref.shape[-1]
    blk = gated_refs[0].shape[0]
    rows_img = tile // img_w
    has_rows = n_tiles * rows_img > 1
    bufs = ((a0_ref, val0_ref), (a1_ref, val1_ref))

    sub = lax.broadcasted_iota(jnp.int32, (SUBLANE, LANE), 0)
    not_first = sub >= 1
    not_last = sub <= SUBLANE - 2

    def up_pieces(a_dst, val_dst):
        def piece(rows, col, dst, cast):
            def run():
                res = _dot(h_ref[0, rows, :], wu_ref[:, col:col + UP_SLAB])
                dst[rows, col % dff:col % dff + UP_SLAB] = res.astype(BF16) if cast else res
            return run
        pieces = []
        for j in range(tile // blk):
            rows = slice(j * blk, (j + 1) * blk)
            pieces += [piece(rows, col, a_dst, False) for col in range(0, dff, UP_SLAB)]
            pieces += [piece(rows, dff + col, val_dst, True) for col in range(0, dff, UP_SLAB)]
        return pieces

    def conv_strip(cur, val, below, r, c, gated, out_rows):
        tok = slice(r * img_w, (r + 1) * img_w)
        cols = slice(c * LANE, (c + 1) * LANE)
        srcs = [(1, cur, tok)]
        if has_rows:
            srcs.append((0, cur, slice((r - 1) * img_w, r * img_w)) if r > 0 else (0, above_ref, slice(0, img_w)))
            if r < rows_img - 1:
                srcs.append((2, cur, slice((r + 1) * img_w, (r + 2) * img_w)))
            elif below is not None:
                srcs.append((2, below, slice(0, img_w)))
        acc = [None, None, None]
        for kr, ref, rws in srcs:
            src = ref[rws, cols]
            for dc in (-1, 0, 1):
                tap = cw_ref[kr * CONV_K + dc + 1:kr * CONV_K + dc + 2, cols]
                acc[dc + 1] = tap * src if acc[dc + 1] is None else acc[dc + 1] + tap * src
        n_sl = img_w // SUBLANE
        dn = [pltpu.roll(acc[0][j * SUBLANE:(j + 1) * SUBLANE], 1, 0) for j in range(n_sl)]
        upw = [pltpu.roll(acc[2][j * SUBLANE:(j + 1) * SUBLANE], SUBLANE - 1, 0) for j in range(n_sl)]
        left = jnp.concatenate(
            [jnp.where(not_first, dn[j], dn[j - 1] if j > 0 else 0.0) for j in range(n_sl)], axis=0)
        right = jnp.concatenate(
            [jnp.where(not_last, upw[j], upw[j + 1] if j < n_sl - 1 else 0.0) for j in range(n_sl)], axis=0)
        conv = acc[1] + left + right + cb_ref[:, cols]
        gated[out_rows, cols] = (_gelu(conv) * val[tok, cols].astype(F32)).astype(BF16)

    def conv_down(cur, val, below, pieces=()):
        slabs = dff // UP_SLAB
        total = len(gated_refs) * slabs
        done = issued = 0
        for j, gated in enumerate(gated_refs):
            y = None
            for k in range(slabs):
                for rr in range(blk // img_w):
                    for c in range(k * (UP_SLAB // LANE), (k + 1) * (UP_SLAB // LANE)):
                        conv_strip(cur, val, below, j * (blk // img_w) + rr, c, gated,
                                   slice(rr * img_w, (rr + 1) * img_w))
                done += 1
                while issued * total < done * len(pieces):
                    pieces[issued]()
                    issued += 1
                ks = slice(k * UP_SLAB, (k + 1) * UP_SLAB)
                part = _dot(gated[:, ks], wd_ref[ks, :d])
                y = part if y is None else y + part
            rows = slice(j * blk, (j + 1) * blk)
            x2 = x_ref[0, rows, :] + g2_ref[0] * y
            if final_norm:
                ms = jnp.mean(x2 * x2, axis=-1, keepdims=True)
                x2 = x2 * lax.rsqrt(ms + EPS) * fg_ref[...]
            o_ref[0, rows, :] = x2
        if has_rows:
            above_ref[...] = cur[tile - img_w:tile, :]

    @pl.when(s == 0)
    def _():
        if has_rows:
            above_ref[...] = jnp.zeros(above_ref.shape, F32)
        for run in up_pieces(*bufs[0]):
            run()

    for parity in sorted({step % 2 for step in range(1, n_tiles)}):
        @pl.when(jnp.logical_and(jnp.logical_and(s > 0, s < n_tiles), lax.rem(s, 2) == parity))
        def _():
            new, (cur, val) = bufs[parity], bufs[1 - parity]
            conv_down(cur, val, new[0], up_pieces(*new))

    @pl.when(s == n_tiles)
    def _():
        conv_down(*bufs[(n_tiles - 1) % 2], None)


def _ffn(h2, x1, w_up, conv_w, conv_b, w_down, g2, final_g, *, tile, img_w, final_norm):
    bsz, length, d = x1.shape
    dff = w_down.shape[0]
    n_tiles = length // tile
    assert n_tiles > 1 or tile == img_w, "a single tile must be a single image row"
    blk = max(FFN_DOWN_BLOCK, img_w)
    up_tile = lambda b, s: (b, jnp.minimum(s, n_tiles - 1), 0)
    down_tile = lambda b, s: (b, jnp.maximum(s - 1, 0), 0)
    return pl.pallas_call(
        functools.partial(_ffn_kernel, n_tiles=n_tiles, tile=tile, img_w=img_w, final_norm=final_norm),
        out_shape=jax.ShapeDtypeStruct(x1.shape, F32),
        grid=(bsz, n_tiles + 1),
        in_specs=[
            pl.BlockSpec((1, tile, d), up_tile),
            pl.BlockSpec((1, tile, d), down_tile),
            _const_spec(w_up.shape),
            _const_spec(conv_w.shape),
            _const_spec(conv_b.shape),
            _const_spec(w_down.shape),
            pl.BlockSpec((1, 1, d), lambda b, s: (b, 0, 0)),
            _const_spec((1, d)),
        ],
        out_specs=pl.BlockSpec((1, tile, d), down_tile),
        scratch_shapes=[pltpu.VMEM((tile, dff), F32), pltpu.VMEM((tile, dff), F32),
                        pltpu.VMEM((tile, dff), BF16), pltpu.VMEM((tile, dff), BF16),
                        pltpu.VMEM((img_w, dff), F32)]
        + [pltpu.VMEM((blk, dff), BF16) for _ in range(tile // blk)],
        compiler_params=_params("parallel", "arbitrary"),
        name="ffn",
    )(h2, x1, w_up, conv_w, conv_b, w_down, g2, final_g)


def _stream_layer(x, mods, wts, s0f, s0b, *, tm, gla_tile, ffn_tile, img_w, states_only, final_norm):
    sh1, sc1, g1, sh2, sc2, g2 = mods
    gu, vn, v, rs, sga, sgb, q, k, la = _in_proj(
        x, sh1, sc1, wts["norm1_g"], wts["w_all"], wts["w2cat"], wts["b2cat"],
        wts["sgu_ln_g"], wts["sgu_ln_b"], tm=tm)
    o_f, o_b, s_f, s_b = _gla(q, k, v, la, s0f, s0b, tile=gla_tile)
    if states_only:
        return None, s_f, s_b
    x1, h2 = _mix(x, gu, vn, o_f, o_b, rs, sga, sgb, wts["sgu_w"], wts["sgu_b_full"], wts["gla_norm_g"],
                  wts["w_br_a"], wts["w_br_b"], wts["w_o"], g1, wts["norm2_g"], sh2, sc2, tm=tm)
    x2 = _ffn(h2, x1, wts["ffn_w_up"], wts["ffn_conv_w"], wts["ffn_conv_b"], wts["ffn_w_down"], g2, wts["final_g"],
              tile=ffn_tile, img_w=img_w, final_norm=final_norm)
    return x2, s_f, s_b


def kernel(x, c, ctx, c_ctx, w_ada, b_ada, norm1_g, norm2_g, w_in, sgu_ln_g, sgu_ln_b, sgu_w, sgu_b, gla_w2, gla_b2, gla_norm_g, w_br_a, w_br_b, w_o, ffn_w_up, ffn_conv_w, ffn_conv_b, ffn_w_down, final_norm_g):
    bsz, seq, d = x.shape
    ctx_len = ctx.shape[1]
    depth = w_in.shape[0]
    kd = d // 2
    hk, hv = kd // GLA_HEADS, d // GLA_HEADS
    dff = ffn_w_down.shape[1]

    n_rows = -(-(bsz + 1) // SUBLANE) * SUBLANE
    cc = jnp.zeros((n_rows, d), F32).at[:bsz].set(c).at[bsz].set(c_ctx)
    mods = _ada_mods(cc, w_ada, b_ada)

    w_all = _bf16_odd_tiles(w_in)
    w2cat = jnp.zeros((depth, LANE, 2 * kd), F32)
    w2cat = w2cat.at[:, :GLA_RANK, :kd].set(gla_w2[:, 0]).at[:, GLA_RANK:2 * GLA_RANK, kd:].set(gla_w2[:, 1])
    w2cat = w2cat.astype(BF16)
    b2cat = gla_b2.reshape(depth, 1, 2 * kd)
    sgu_b_full = jnp.repeat(jnp.swapaxes(sgu_b, 1, 2), LANE, axis=2)

    zero_state = jnp.zeros((bsz, GLA_HEADS, hk, hv), F32)
    tm = 512
    for l in range(depth):
        last = l == depth - 1
        wts = {
            "norm1_g": norm1_g[l][None], "norm2_g": norm2_g[l][None],
            "w_all": w_all[l], "w2cat": w2cat[l], "b2cat": b2cat[l],
            "sgu_ln_g": sgu_ln_g[l][None], "sgu_ln_b": sgu_ln_b[l][None],
            "sgu_w": sgu_w[l].astype(BF16), "sgu_b_full": sgu_b_full[l],
            "gla_norm_g": gla_norm_g[l][None],
            "w_br_a": _bf16_odd_tiles(w_br_a[l]), "w_br_b": _bf16_odd_tiles(w_br_b[l]),
            "w_o": _bf16_odd_tiles(w_o[l]),
            "ffn_w_up": ffn_w_up[l].astype(BF16), "ffn_conv_w": ffn_conv_w[l].reshape(CONV_K * CONV_K, dff),
            "ffn_conv_b": ffn_conv_b[l][None], "ffn_w_down": _bf16_odd_tiles(ffn_w_down[l]),
            "final_g": final_norm_g[None],
        }
        m = mods[l].reshape(n_rows, N_MOD, d)
        mods_x = [m[:bsz, j][:, None, :] for j in range(N_MOD)]
        mods_c = [jnp.broadcast_to(m[bsz, j][None, None, :], (bsz, 1, d)) for j in range(N_MOD)]
        ctx, s_f, s_b = _stream_layer(
            ctx, mods_c, wts, zero_state, zero_state, tm=min(tm, ctx_len), gla_tile=ctx_len, ffn_tile=ctx_len,
            img_w=ctx_len, states_only=last, final_norm=False)
        x, _, _ = _stream_layer(
            x, mods_x, wts, s_f, s_b, tm=tm, gla_tile=1024, ffn_tile=512, img_w=GRID_W,
            states_only=False, final_norm=last)
    return x
```

```python
import functools

import jax
import jax.numpy as jnp
from jax import lax
from jax.experimental import pallas as pl
from jax.experimental.pallas import tpu as pltpu

F32 = jnp.float32
BF16 = jnp.bfloat16

EPS = 1e-6
N_MOD = 6
SGU_CHUNK = 128
SGU_GROUPS = 8
GLA_HEADS = 4
GLA_RANK = 16
GLA_GATE_NORMALIZER = 16.0
GLA_CHUNK = 64
GRID_W = 64
CONV_K = 3
LANE = 128
SUBLANE = 8
VMEM_LIMIT_BYTES = 56 * 1024 * 1024
INV_SQRT2 = 0.7071067811865476
FFN_DOWN_BLOCK = 256
UP_SLAB = 256


def _params(*semantics):
    return pltpu.CompilerParams(dimension_semantics=semantics, vmem_limit_bytes=VMEM_LIMIT_BYTES)


def _const_spec(shape):
    zeros = (0,) * len(shape)
    return pl.BlockSpec(shape, lambda *_: zeros, pipeline_mode=pl.Buffered(1))


def _gelu(v):
    return 0.5 * v * (1.0 + lax.erf(v * INV_SQRT2))


def _sigmoid(v):
    return 1.0 / (1.0 + jnp.exp(-v))


def _log_sigmoid(v):
    return jnp.minimum(v, 0.0) - jnp.log(1.0 + jnp.exp(-jnp.abs(v)))


def _rms_mod(x, g, shift, scale):
    ms = jnp.mean(x * x, axis=-1, keepdims=True)
    y = x * lax.rsqrt(ms + EPS) * g
    return y * (1.0 + scale) + shift


def _dot(a, b):
    return jnp.dot(a, b, preferred_element_type=F32)


def _dot_nt(a, b):
    return lax.dot_general(a, b, (((1,), (1,)), ((), ())), preferred_element_type=F32)


def _bf16_odd_tiles(w):
    tiles = -(-w.shape[-1] // LANE)
    tiles += 1 - tiles % 2
    pad = [(0, 0)] * (w.ndim - 1) + [(0, tiles * LANE - w.shape[-1])]
    return jnp.pad(w.astype(BF16), pad)


def _ada_kernel(c_ref, w_ref, b_ref, o_ref):
    c = c_ref[...]
    s = (c * _sigmoid(c)).astype(BF16)
    o_ref[0] = _dot(s, w_ref[0].astype(BF16)) + b_ref[0]


def _ada_mods(cc, w_ada, b_ada):
    depth, d, n = w_ada.shape
    tn = n // 4
    return pl.pallas_call(
        _ada_kernel,
        out_shape=jax.ShapeDtypeStruct((depth, cc.shape[0], n), F32),
        grid=(depth, n // tn),
        in_specs=[
            pl.BlockSpec(cc.shape, lambda l, j: (0, 0)),
            pl.BlockSpec((1, d, tn), lambda l, j: (l, 0, j)),
            pl.BlockSpec((1, 1, tn), lambda l, j: (l, 0, j)),
        ],
        out_specs=pl.BlockSpec((1, cc.shape[0], tn), lambda l, j: (l, 0, j)),
        compiler_params=_params("parallel", "parallel"),
        name="ada_mods",
    )(cc, w_ada, b_ada.reshape(depth, 1, n))


def _chunk_cumsum(x, reverse):
    n_sl = x.shape[0] // SUBLANE
    sub = lax.broadcasted_iota(jnp.int32, (SUBLANE, x.shape[1]), 0)
    tiles = []
    for j in range(n_sl):
        y = x[j * SUBLANE:(j + 1) * SUBLANE]
        for s in (1, 2, 4):
            if reverse:
                y = y + jnp.where(sub < SUBLANE - s, pltpu.roll(y, SUBLANE - s, 0), 0.0)
            else:
                y = y + jnp.where(sub >= s, pltpu.roll(y, s, 0), 0.0)
        tiles.append(y)
    edge = 0 if reverse else SUBLANE - 1
    carry = None
    for j in (range(n_sl - 1, -1, -1) if reverse else range(n_sl)):
        if carry is not None:
            tiles[j] = tiles[j] + carry
        carry = jnp.broadcast_to(tiles[j][edge:edge + 1], tiles[j].shape)
    return jnp.concatenate(tiles, axis=0)


def _inproj_kernel(x_ref, sh_ref, sc_ref, g_ref, w_ref, w2_ref, b2_ref, lng_ref, lnb_ref,
                   gu_ref, vn_ref, v_ref, rs_ref, sga_ref, sgb_ref, q_ref, k_ref, la_ref, hb_ref,
                   *, d, kd, q_scale):
    hb_ref[...] = _rms_mod(x_ref[0], g_ref[...], sh_ref[0], sc_ref[0]).astype(BF16)

    def seg(lo, width):
        return _dot(hb_ref[...], w_ref[:, lo:lo + width])

    off_u, off_vs, off_q, off_r = 0, d, 2 * d, 2 * d + kd
    off_ga, off_gb, off_k, off_v = 3 * d + kd, 4 * d + kd, 5 * d + kd, 5 * d + 2 * kd
    off_lr = 6 * d + 2 * kd

    lr = seg(off_lr, LANE)
    q = seg(off_q, kd)
    lr = lr.astype(BF16)
    k = seg(off_k, kd)
    q_ref[0] = q * q_scale
    z = _dot(lr, w2_ref[...]) + b2_ref[...]
    k_ref[0] = k
    ga = seg(off_ga, d)
    la = _log_sigmoid(z) * (1.0 / GLA_GATE_NORMALIZER)
    for ci in range(x_ref.shape[1] // GLA_CHUNK):
        rows = slice(ci * GLA_CHUNK, (ci + 1) * GLA_CHUNK)
        la_ref[0, rows, :kd] = _chunk_cumsum(la[rows, :kd], False)
        la_ref[0, rows, kd:] = _chunk_cumsum(la[rows, kd:], True)
    gb = seg(off_gb, d)
    sga_ref[0] = _sigmoid(ga).astype(BF16)
    r = seg(off_r, d)
    sgb_ref[0] = _sigmoid(gb).astype(BF16)
    u = seg(off_u, d)
    rs_ref[0] = (r * _sigmoid(r)).astype(BF16)
    vs = seg(off_vs, d)
    gu_ref[0] = _gelu(u).astype(BF16)
    v = seg(off_v, d)
    gv = _gelu(vs)
    mu = jnp.mean(gv, axis=-1, keepdims=True)
    cen = gv - mu
    var = jnp.mean(cen * cen, axis=-1, keepdims=True)
    vn_ref[0] = (cen * lax.rsqrt(var + EPS) * lng_ref[...] + lnb_ref[...]).astype(BF16)
    v_ref[0] = v.astype(BF16)


def _in_proj(x, shift, scale, norm_g, w_all, w2cat, b2cat, ln_g, ln_b, *, tm):
    bsz, length, d = x.shape
    kd = d // 2
    hk = kd // GLA_HEADS
    row = lambda b, i: (b, i, 0)
    mod = lambda b, i: (b, 0, 0)
    wide = jax.ShapeDtypeStruct((bsz, length, d), BF16)
    half = jax.ShapeDtypeStruct((bsz, length, kd), F32)
    wide_spec = pl.BlockSpec((1, tm, d), row)
    half_spec = pl.BlockSpec((1, tm, kd), row)
    return pl.pallas_call(
        functools.partial(_inproj_kernel, d=d, kd=kd, q_scale=float(hk) ** -0.5),
        out_shape=(wide,) * 6 + (half, half, jax.ShapeDtypeStruct((bsz, length, 2 * kd), F32)),
        grid=(bsz, length // tm),
        in_specs=[
            pl.BlockSpec((1, tm, d), row),
            pl.BlockSpec((1, 1, d), mod),
            pl.BlockSpec((1, 1, d), mod),
            _const_spec((1, d)),
            _const_spec(w_all.shape),
            _const_spec(w2cat.shape),
            _const_spec(b2cat.shape),
            _const_spec((1, d)),
            _const_spec((1, d)),
        ],
        out_specs=(wide_spec,) * 6 + (half_spec, half_spec, pl.BlockSpec((1, tm, 2 * kd), row)),
        scratch_shapes=[pltpu.VMEM((tm, d), BF16)],
        compiler_params=_params("parallel", "parallel"),
        name="in_proj",
    )(x, shift, scale, norm_g, w_all, w2cat, b2cat, ln_g, ln_b)


def _gla_kernel(qf_ref, kf_ref, laf_ref, vf_ref, qb_ref, kb_ref, lab_ref, vb_ref, s0f_ref, s0b_ref,
                of_ref, ob_ref, sf_ref, sb_ref, *, n_chunks, hk, hv):
    @pl.when(pl.program_id(1) == 0)
    def _():
        sf_ref[...] = s0f_ref[...]
        sb_ref[...] = s0b_ref[...]

    r = lax.broadcasted_iota(jnp.int32, (GLA_CHUNK, GLA_CHUNK), 0)
    c = lax.broadcasted_iota(jnp.int32, (GLA_CHUNK, GLA_CHUNK), 1)
    fwd = (qf_ref, kf_ref, laf_ref, vf_ref, of_ref, sf_ref, c <= r, False)
    bwd = (qb_ref, kb_ref, lab_ref, vb_ref, ob_ref, sb_ref, c >= r, True)

    def decayed_operands(refs, ci):
        q_ref, k_ref, b_ref, _, _, _, _, reverse = refs
        rows = slice(ci * GLA_CHUNK, (ci + 1) * GLA_CHUNK)
        b = b_ref[0, rows, :]
        b_last = b[0:1] if reverse else b[GLA_CHUNK - 1:GLA_CHUNK]
        k = k_ref[0, rows, :]
        qd = (q_ref[0, rows, :] * jnp.exp(b)).astype(BF16)
        ki = (k * jnp.exp(-b)).astype(BF16)
        kdt = (k * jnp.exp(b_last - b)).T.astype(BF16)
        dec = jnp.broadcast_to(jnp.exp(b_last), (SUBLANE, b.shape[1])).T
        return qd, ki, kdt, dec

    def chunks_of(step):
        return ((fwd, step), (bwd, n_chunks - 1 - step))

    ops = [decayed_operands(refs, ci) for refs, ci in chunks_of(0)]
    for step in range(n_chunks):
        streams = [(refs, ci, h, op) for (refs, ci), op in zip(chunks_of(step), ops) for h in range(GLA_HEADS)]
        raw, kvs = [], []
        for (_, _, _, v_ref, _, _, _, _), ci, h, (qd, ki, kdt, _) in streams:
            rows = slice(ci * GLA_CHUNK, (ci + 1) * GLA_CHUNK)
            ks = slice(h * hk, (h + 1) * hk)
            raw.append(_dot_nt(qd[:, ks], ki[:, ks]))
            kvs.append(_dot(kdt[ks, :], v_ref[0, rows, h * hv:(h + 1) * hv]))
        if step + 1 < n_chunks:
            ops = [decayed_operands(refs, ci) for refs, ci in chunks_of(step + 1)]
        for ((_, _, _, v_ref, o_ref, st_ref, keep, _), ci, h, (qd, _, _, dec)), sc, kv in zip(streams, raw, kvs):
            rows = slice(ci * GLA_CHUNK, (ci + 1) * GLA_CHUNK)
            ks = slice(h * hk, (h + 1) * hk)
            vs = slice(h * hv, (h + 1) * hv)
            scores = jnp.where(keep, sc, 0.0).astype(BF16)
            state = st_ref[0, h]
            o_inter = _dot(qd[:, ks], state.astype(BF16))
            o_ref[0, rows, vs] = (_dot(scores, v_ref[0, rows, vs]) + o_inter).astype(BF16)
            st_ref[0, h] = state * jnp.broadcast_to(dec[ks, 0:1], (hk, hv)) + kv


def _gla(q, k, v, la, s0f, s0b, *, tile):
    bsz, length, kd = q.shape
    vd = v.shape[-1]
    hk, hv = kd // GLA_HEADS, vd // GLA_HEADS
    nt = length // tile
    fwd = lambda b, i: (b, i, 0)
    bwd = lambda b, i: (b, nt - 1 - i, 0)
    st_spec = pl.BlockSpec((1, GLA_HEADS, hk, hv), lambda b, i: (b, 0, 0, 0))
    out = jax.ShapeDtypeStruct((bsz, length, vd), BF16)

    def direction_specs(idx, la_col):
        return [
            pl.BlockSpec((1, tile, kd), idx),
            pl.BlockSpec((1, tile, kd), idx),
            pl.BlockSpec((1, tile, kd), lambda b, i: (b, idx(b, i)[1], la_col)),
            pl.BlockSpec((1, tile, vd), idx),
        ]

    return pl.pallas_call(
        functools.partial(_gla_kernel, n_chunks=tile // GLA_CHUNK, hk=hk, hv=hv),
        out_shape=(out, out, jax.ShapeDtypeStruct(s0f.shape, F32), jax.ShapeDtypeStruct(s0b.shape, F32)),
        grid=(bsz, nt),
        in_specs=direction_specs(fwd, 0) + direction_specs(bwd, 1) + [st_spec, st_spec],
        out_specs=(pl.BlockSpec((1, tile, vd), fwd), pl.BlockSpec((1, tile, vd), bwd), st_spec, st_spec),
        compiler_params=_params("parallel", "arbitrary"),
        name="gla_scan",
    )(q, k, la, v, q, k, la, v, s0f, s0b)


def _mix_kernel(x_ref, gu_ref, vn_ref, of_ref, ob_ref, rs_ref, sga_ref, sgb_ref, ws_ref, bs_ref, gn_ref,
                wa_ref, wb_ref, wo_ref, g1_ref, n2_ref, sh2_ref, sc2_ref, x1_ref, h2_ref,
                s_buf, a_buf, b_buf, m_buf, *, tm, hv):
    d = x_ref.shape[-1]
    for n in range(tm // SGU_CHUNK):
        rows = slice(n * SGU_CHUNK, (n + 1) * SGU_CHUNK)
        for g in range(SGU_GROUPS):
            cols = slice(g * LANE, (g + 1) * LANE)
            s_buf[rows, cols] = _dot(ws_ref[g], vn_ref[0, rows, cols]) + bs_ref[:, cols]
    for h in range(GLA_HEADS):
        cols = slice(h * hv, (h + 1) * hv)
        oh = of_ref[0, :, cols].astype(F32) + ob_ref[0, :, cols].astype(F32)
        ms = jnp.mean(oh * oh, axis=-1, keepdims=True)
        on = oh * lax.rsqrt(ms + EPS) * gn_ref[:, cols]
        b_buf[:, cols] = (on * rs_ref[0, :, cols].astype(F32)).astype(BF16)
    proj_b = _dot(b_buf[...], wb_ref[:, :d])
    a_buf[...] = (gu_ref[0].astype(F32) * s_buf[...]).astype(BF16)
    proj_a = _dot(a_buf[...], wa_ref[:, :d])
    m_buf[...] = (sga_ref[0].astype(F32) * proj_a + sgb_ref[0].astype(F32) * proj_b).astype(BF16)
    x1 = x_ref[0] + g1_ref[0] * _dot(m_buf[...], wo_ref[:, :d])
    x1_ref[0] = x1
    h2_ref[0] = _rms_mod(x1, n2_ref[...], sh2_ref[0], sc2_ref[0]).astype(BF16)


def _mix(x, gu, vn, o_f, o_b, rs, sga, sgb, ws, bs_full, gn_g, wa, wb, wo, g1, n2_g, sh2, sc2, *, tm):
    bsz, length, d = x.shape
    row = lambda b, i: (b, i, 0)
    mod = lambda b, i: (b, 0, 0)
    tile = pl.BlockSpec((1, tm, d), row)
    modv = pl.BlockSpec((1, 1, d), mod)
    return pl.pallas_call(
        functools.partial(_mix_kernel, tm=tm, hv=d // GLA_HEADS),
        out_shape=(jax.ShapeDtypeStruct((bsz, length, d), F32), jax.ShapeDtypeStruct((bsz, length, d), BF16)),
        grid=(bsz, length // tm),
        in_specs=[tile] * 8 + [
            _const_spec(ws.shape),
            _const_spec(bs_full.shape),
            _const_spec((1, d)),
            _const_spec(wa.shape),
            _const_spec(wb.shape),
            _const_spec(wo.shape),
            modv,
            _const_spec((1, d)),
            modv,
            modv,
        ],
        out_specs=(tile, tile),
        scratch_shapes=[pltpu.VMEM((tm, d), F32)] + [pltpu.VMEM((tm, d), BF16) for _ in range(3)],
        compiler_params=_params("parallel", "parallel"),
        name="mix",
    )(x, gu, vn, o_f, o_b, rs, sga, sgb, ws, bs_full, gn_g, wa, wb, wo, g1, n2_g, sh2, sc2)


def _ffn_kernel(h_ref, x_ref, wu_ref, cw_ref, cb_ref, wd_ref, g2_ref, fg_ref, o_ref,
                a0_ref, a1_ref, val0_ref, val1_ref, above_ref, *gated_refs,
                n_tiles, tile, img_w, final_norm):
    s = pl.program_id(1)
    dff = a0_ref.shape[-1]
    d = x_ref.shape[-1]
    blk = gated_refs[0].shape[0]
    rows_img = tile // img_w
    has_rows = n_tiles * rows_img > 1
    bufs = ((a0_ref, val0_ref), (a1_ref, val1_ref))

    sub = lax.broadcasted_iota(jnp.int32, (SUBLANE, LANE), 0)
    not_first = sub >= 1
    not_last = sub <= SUBLANE - 2

    def up_pieces(a_dst, val_dst):
        def piece(rows, col, dst, cast):
            def run():
                res = _dot(h_ref[0, rows, :], wu_ref[:, col:col + UP_SLAB])
                dst[rows, col % dff:col % dff + UP_SLAB] = res.astype(BF16) if cast else res
            return run
        pieces = []
        for j in range(tile // blk):
            rows = slice(j * blk, (j + 1) * blk)
            pieces += [piece(rows, col, a_dst, False) for col in range(0, dff, UP_SLAB)]
            pieces += [piece(rows, dff + col, val_dst, True) for col in range(0, dff, UP_SLAB)]
        return pieces

    def conv_strip(cur, val, below, r, c, gated, out_rows):
        tok = slice(r * img_w, (r + 1) * img_w)
        cols = slice(c * LANE, (c + 1) * LANE)
        srcs = [(1, cur, tok)]
        if has_rows:
            srcs.append((0, cur, slice((r - 1) * img_w, r * img_w)) if r > 0 else (0, above_ref, slice(0, img_w)))
            if r < rows_img - 1:
                srcs.append((2, cur, slice((r + 1) * img_w, (r + 2) * img_w)))
            elif below is not None:
                srcs.append((2, below, slice(0, img_w)))
        acc = [None, None, None]
        for kr, ref, rws in srcs:
            src = ref[rws, cols]
            for dc in (-1, 0, 1):
                tap = cw_ref[kr * CONV_K + dc + 1:kr * CONV_K + dc + 2, cols]
                acc[dc + 1] = tap * src if acc[dc + 1] is None else acc[dc + 1] + tap * src
        n_sl = img_w // SUBLANE
        dn = [pltpu.roll(acc[0][j * SUBLANE:(j + 1) * SUBLANE], 1, 0) for j in range(n_sl)]
        upw = [pltpu.roll(acc[2][j * SUBLANE:(j + 1) * SUBLANE], SUBLANE - 1, 0) for j in range(n_sl)]
        left = jnp.concatenate(
            [jnp.where(not_first, dn[j], dn[j - 1] if j > 0 else 0.0) for j in range(n_sl)], axis=0)
        right = jnp.concatenate(
            [jnp.where(not_last, upw[j], upw[j + 1] if j < n_sl - 1 else 0.0) for j in range(n_sl)], axis=0)
        conv = acc[1] + left + right + cb_ref[:, cols]
        gated[out_rows, cols] = (_gelu(conv) * val[tok, cols].astype(F32)).astype(BF16)

    def conv_down(cur, val, below, pieces=()):
        slabs = dff // UP_SLAB
        total = len(gated_refs) * slabs
        assert not pieces or len(pieces) == 2 * total
        done = issued = 0
        for j, gated in enumerate(gated_refs):
            y = None
            for k in range(slabs):
                done += 1
                while issued * total < done * len(pieces):
                    pieces[issued]()
                    issued += 1
                for rr in range(blk // img_w):
                    for c in range(k * (UP_SLAB // LANE), (k + 1) * (UP_SLAB // LANE)):
                        conv_strip(cur, val, below, j * (blk // img_w) + rr, c, gated,
                                   slice(rr * img_w, (rr + 1) * img_w))
                ks = slice(k * UP_SLAB, (k + 1) * UP_SLAB)
                part = _dot(gated[:, ks], wd_ref[ks, :d])
                y = part if y is None else y + part
            rows = slice(j * blk, (j + 1) * blk)
            x2 = x_ref[0, rows, :] + g2_ref[0] * y
            if final_norm:
                ms = jnp.mean(x2 * x2, axis=-1, keepdims=True)
                x2 = x2 * lax.rsqrt(ms + EPS) * fg_ref[...]
            o_ref[0, rows, :] = x2
        if has_rows:
            above_ref[...] = cur[tile - img_w:tile, :]

    @pl.when(s == 0)
    def _():
        if has_rows:
            above_ref[...] = jnp.zeros(above_ref.shape, F32)
        for run in up_pieces(*bufs[0]):
            run()

    for parity in sorted({step % 2 for step in range(1, n_tiles)}):
        @pl.when(jnp.logical_and(jnp.logical_and(s > 0, s < n_tiles), lax.rem(s, 2) == parity))
        def _():
            new, (cur, val) = bufs[parity], bufs[1 - parity]
            conv_down(cur, val, new[0], up_pieces(*new))

    @pl.when(s == n_tiles)
    def _():
        conv_down(*bufs[(n_tiles - 1) % 2], None)


def _ffn(h2, x1, w_up, conv_w, conv_b, w_down, g2, final_g, *, tile, img_w, final_norm):
    bsz, length, d = x1.shape
    dff = w_down.shape[0]
    n_tiles = length // tile
    assert n_tiles > 1 or tile == img_w, "a single tile must be a single image row"
    blk = max(FFN_DOWN_BLOCK, img_w)
    up_tile = lambda b, s: (b, jnp.minimum(s, n_tiles - 1), 0)
    down_tile = lambda b, s: (b, jnp.maximum(s - 1, 0), 0)
    return pl.pallas_call(
        functools.partial(_ffn_kernel, n_tiles=n_tiles, tile=tile, img_w=img_w, final_norm=final_norm),
        out_shape=jax.ShapeDtypeStruct(x1.shape, F32),
        grid=(bsz, n_tiles + 1),
        in_specs=[
            pl.BlockSpec((1, tile, d), up_tile),
            pl.BlockSpec((1, tile, d), down_tile),
            _const_spec(w_up.shape),
            _const_spec(conv_w.shape),
            _const_spec(conv_b.shape),
            _const_spec(w_down.shape),
            pl.BlockSpec((1, 1, d), lambda b, s: (b, 0, 0)),
            _const_spec((1, d)),
        ],
        out_specs=pl.BlockSpec((1, tile, d), down_tile),
        scratch_shapes=[pltpu.VMEM((tile, dff), F32), pltpu.VMEM((tile, dff), F32),
                        pltpu.VMEM((tile, dff), BF16), pltpu.VMEM((tile, dff), BF16),
                        pltpu.VMEM((img_w, dff), F32)]
        + [pltpu.VMEM((blk, dff), BF16) for _ in range(tile // blk)],
        compiler_params=_params("parallel", "arbitrary"),
        name="ffn",
    )(h2, x1, w_up, conv_w, conv_b, w_down, g2, final_g)


def _stream_layer(x, mods, wts, s0f, s0b, *, tm, gla_tile, ffn_tile, img_w, states_only, final_norm):
    sh1, sc1, g1, sh2, sc2, g2 = mods
    gu, vn, v, rs, sga, sgb, q, k, la = _in_proj(
        x, sh1, sc1, wts["norm1_g"], wts["w_all"], wts["w2cat"], wts["b2cat"],
        wts["sgu_ln_g"], wts["sgu_ln_b"], tm=tm)
    o_f, o_b, s_f, s_b = _gla(q, k, v, la, s0f, s0b, tile=gla_tile)
    if states_only:
        return None, s_f, s_b
    x1, h2 = _mix(x, gu, vn, o_f, o_b, rs, sga, sgb, wts["sgu_w"], wts["sgu_b_full"], wts["gla_norm_g"],
                  wts["w_br_a"], wts["w_br_b"], wts["w_o"], g1, wts["norm2_g"], sh2, sc2, tm=tm)
    x2 = _ffn(h2, x1, wts["ffn_w_up"], wts["ffn_conv_w"], wts["ffn_conv_b"], wts["ffn_w_down"], g2, wts["final_g"],
              tile=ffn_tile, img_w=img_w, final_norm=final_norm)
    return x2, s_f, s_b


def kernel(x, c, ctx, c_ctx, w_ada, b_ada, norm1_g, norm2_g, w_in, sgu_ln_g, sgu_ln_b, sgu_w, sgu_b, gla_w2, gla_b2, gla_norm_g, w_br_a, w_br_b, w_o, ffn_w_up, ffn_conv_w, ffn_conv_b, ffn_w_down, final_norm_g):
    bsz, seq, d = x.shape
    ctx_len = ctx.shape[1]
    depth = w_in.shape[0]
    kd = d // 2
    hk, hv = kd // GLA_HEADS, d // GLA_HEADS
    dff = ffn_w_down.shape[1]

    n_rows = -(-(bsz + 1) // SUBLANE) * SUBLANE
    cc = jnp.zeros((n_rows, d), F32).at[:bsz].set(c).at[bsz].set(c_ctx)
    mods = _ada_mods(cc, w_ada, b_ada)

    w_all = _bf16_odd_tiles(w_in)
    w2cat = jnp.zeros((depth, LANE, 2 * kd), F32)
    w2cat = w2cat.at[:, :GLA_RANK, :kd].set(gla_w2[:, 0]).at[:, GLA_RANK:2 * GLA_RANK, kd:].set(gla_w2[:, 1])
    w2cat = w2cat.astype(BF16)
    b2cat = gla_b2.reshape(depth, 1, 2 * kd)
    sgu_b_full = jnp.repeat(jnp.swapaxes(sgu_b, 1, 2), LANE, axis=2)

    zero_state = jnp.zeros((bsz, GLA_HEADS, hk, hv), F32)
    tm = 512
    for l in range(depth):
        last = l == depth - 1
        wts = {
            "norm1_g": norm1_g[l][None], "norm2_g": norm2_g[l][None],
            "w_all": w_all[l], "w2cat": w2cat[l], "b2cat": b2cat[l],
            "sgu_ln_g": sgu_ln_g[l][None], "sgu_ln_b": sgu_ln_b[l][None],
            "sgu_w": sgu_w[l].astype(BF16), "sgu_b_full": sgu_b_full[l],
            "gla_norm_g": gla_norm_g[l][None],
            "w_br_a": _bf16_odd_tiles(w_br_a[l]), "w_br_b": _bf16_odd_tiles(w_br_b[l]),
            "w_o": _bf16_odd_tiles(w_o[l]),
            "ffn_w_up": ffn_w_up[l].astype(BF16), "ffn_conv_w": ffn_conv_w[l].reshape(CONV_K * CONV_K, dff),
            "ffn_conv_b": ffn_conv_b[l][None], "ffn_w_down": _bf16_odd_tiles(ffn_w_down[l]),
            "final_g": final_norm_g[None],
        }
        m = mods[l].reshape(n_rows, N_MOD, d)
        mods_x = [m[:bsz, j][:, None, :] for j in range(N_MOD)]
        mods_c = [jnp.broadcast_to(m[bsz, j][None, None, :], (bsz, 1, d)) for j in range(N_MOD)]
        ctx, s_f, s_b = _stream_layer(
            ctx, mods_c, wts, zero_state, zero_state, tm=min(tm, ctx_len), gla_tile=ctx_len, ffn_tile=ctx_len,
            img_w=ctx_len, states_only=last, final_norm=False)
        x, _, _ = _stream_layer(
            x, mods_x, wts, s_f, s_b, tm=tm, gla_tile=512, ffn_tile=256, img_w=GRID_W,
            states_only=False, final_norm=last)
    return x
```

```python
import functools

import jax
import jax.numpy as jnp
from jax import lax
from jax.experimental import pallas as pl
from jax.experimental.pallas import tpu as pltpu

F32 = jnp.float32
BF16 = jnp.bfloat16

EPS = 1e-6
N_MOD = 6
SGU_CHUNK = 128
SGU_GROUPS = 8
GLA_HEADS = 4
GLA_RANK = 16
GLA_GATE_NORMALIZER = 16.0
GLA_CHUNK = 64
GRID_W = 64
CONV_K = 3
LANE = 128
SUBLANE = 8
VMEM_LIMIT_BYTES = 56 * 1024 * 1024
INV_SQRT2 = 0.7071067811865476
FFN_DOWN_BLOCK = 128
UP_SLAB = 256


def _params(*semantics):
    return pltpu.CompilerParams(dimension_semantics=semantics, vmem_limit_bytes=VMEM_LIMIT_BYTES)


def _const_spec(shape):
    zeros = (0,) * len(shape)
    return pl.BlockSpec(shape, lambda *_: zeros, pipeline_mode=pl.Buffered(1))


def _gelu(v):
    return 0.5 * v * (1.0 + lax.erf(v * INV_SQRT2))


def _sigmoid(v):
    return 1.0 / (1.0 + jnp.exp(-v))


def _log_sigmoid(v):
    return jnp.minimum(v, 0.0) - jnp.log(1.0 + jnp.exp(-jnp.abs(v)))


def _rms_mod(x, g, shift, scale):
    ms = jnp.mean(x * x, axis=-1, keepdims=True)
    y = x * lax.rsqrt(ms + EPS) * g
    return y * (1.0 + scale) + shift


def _dot(a, b):
    return jnp.dot(a, b, preferred_element_type=F32)


def _dot_nt(a, b):
    return lax.dot_general(a, b, (((1,), (1,)), ((), ())), preferred_element_type=F32)


def _bf16_odd_tiles(w):
    tiles = -(-w.shape[-1] // LANE)
    tiles += 1 - tiles % 2
    pad = [(0, 0)] * (w.ndim - 1) + [(0, tiles * LANE - w.shape[-1])]
    return jnp.pad(w.astype(BF16), pad)


def _ada_kernel(c_ref, w_ref, b_ref, o_ref):
    c = c_ref[...]
    s = (c * _sigmoid(c)).astype(BF16)
    o_ref[0] = _dot(s, w_ref[0].astype(BF16)) + b_ref[0]


def _ada_mods(cc, w_ada, b_ada):
    depth, d, n = w_ada.shape
    tn = n // 4
    return pl.pallas_call(
        _ada_kernel,
        out_shape=jax.ShapeDtypeStruct((depth, cc.shape[0], n), F32),
        grid=(depth, n // tn),
        in_specs=[
            pl.BlockSpec(cc.shape, lambda l, j: (0, 0)),
            pl.BlockSpec((1, d, tn), lambda l, j: (l, 0, j)),
            pl.BlockSpec((1, 1, tn), lambda l, j: (l, 0, j)),
        ],
        out_specs=pl.BlockSpec((1, cc.shape[0], tn), lambda l, j: (l, 0, j)),
        compiler_params=_params("parallel", "parallel"),
        name="ada_mods",
    )(cc, w_ada, b_ada.reshape(depth, 1, n))


def _chunk_cumsum(x, reverse):
    n_sl = x.shape[0] // SUBLANE
    sub = lax.broadcasted_iota(jnp.int32, (SUBLANE, x.shape[1]), 0)
    tiles = []
    for j in range(n_sl):
        y = x[j * SUBLANE:(j + 1) * SUBLANE]
        for s in (1, 2, 4):
            if reverse:
                y = y + jnp.where(sub < SUBLANE - s, pltpu.roll(y, SUBLANE - s, 0), 0.0)
            else:
                y = y + jnp.where(sub >= s, pltpu.roll(y, s, 0), 0.0)
        tiles.append(y)
    edge = 0 if reverse else SUBLANE - 1
    carry = None
    for j in (range(n_sl - 1, -1, -1) if reverse else range(n_sl)):
        if carry is not None:
            tiles[j] = tiles[j] + carry
        carry = jnp.broadcast_to(tiles[j][edge:edge + 1], tiles[j].shape)
    return jnp.concatenate(tiles, axis=0)


def _inproj_kernel(x_ref, sh_ref, sc_ref, g_ref, w_ref, w2_ref, b2_ref, lng_ref, lnb_ref,
                   gu_ref, vn_ref, v_ref, rs_ref, sga_ref, sgb_ref, q_ref, k_ref, la_ref, hb_ref,
                   *, d, kd, q_scale):
    hb_ref[...] = _rms_mod(x_ref[0], g_ref[...], sh_ref[0], sc_ref[0]).astype(BF16)

    def seg(lo, width):
        return _dot(hb_ref[...], w_ref[:, lo:lo + width])

    off_u, off_vs, off_q, off_r = 0, d, 2 * d, 2 * d + kd
    off_ga, off_gb, off_k, off_v = 3 * d + kd, 4 * d + kd, 5 * d + kd, 5 * d + 2 * kd
    off_lr = 6 * d + 2 * kd

    lr = seg(off_lr, LANE)
    q = seg(off_q, kd)
    lr = lr.astype(BF16)
    k = seg(off_k, kd)
    q_ref[0] = q * q_scale
    z = _dot(lr, w2_ref[...]) + b2_ref[...]
    k_ref[0] = k
    ga = seg(off_ga, d)
    la = _log_sigmoid(z) * (1.0 / GLA_GATE_NORMALIZER)
    for ci in range(x_ref.shape[1] // GLA_CHUNK):
        rows = slice(ci * GLA_CHUNK, (ci + 1) * GLA_CHUNK)
        la_ref[0, rows, :kd] = _chunk_cumsum(la[rows, :kd], False)
        la_ref[0, rows, kd:] = _chunk_cumsum(la[rows, kd:], True)
    gb = seg(off_gb, d)
    sga_ref[0] = _sigmoid(ga).astype(BF16)
    r = seg(off_r, d)
    sgb_ref[0] = _sigmoid(gb).astype(BF16)
    u = seg(off_u, d)
    rs_ref[0] = (r * _sigmoid(r)).astype(BF16)
    vs = seg(off_vs, d)
    gu_ref[0] = _gelu(u).astype(BF16)
    v = seg(off_v, d)
    gv = _gelu(vs)
    mu = jnp.mean(gv, axis=-1, keepdims=True)
    cen = gv - mu
    var = jnp.mean(cen * cen, axis=-1, keepdims=True)
    vn_ref[0] = (cen * lax.rsqrt(var + EPS) * lng_ref[...] + lnb_ref[...]).astype(BF16)
    v_ref[0] = v.astype(BF16)


def _in_proj(x, shift, scale, norm_g, w_all, w2cat, b2cat, ln_g, ln_b, *, tm):
    bsz, length, d = x.shape
    kd = d // 2
    hk = kd // GLA_HEADS
    row = lambda b, i: (b, i, 0)
    mod = lambda b, i: (b, 0, 0)
    wide = jax.ShapeDtypeStruct((bsz, length, d), BF16)
    half = jax.ShapeDtypeStruct((bsz, length, kd), F32)
    wide_spec = pl.BlockSpec((1, tm, d), row)
    half_spec = pl.BlockSpec((1, tm, kd), row)
    return pl.pallas_call(
        functools.partial(_inproj_kernel, d=d, kd=kd, q_scale=float(hk) ** -0.5),
        out_shape=(wide,) * 6 + (half, half, jax.ShapeDtypeStruct((bsz, length, 2 * kd), F32)),
        grid=(bsz, length // tm),
        in_specs=[
            pl.BlockSpec((1, tm, d), row),
            pl.BlockSpec((1, 1, d), mod),
            pl.BlockSpec((1, 1, d), mod),
            _const_spec((1, d)),
            _const_spec(w_all.shape),
            _const_spec(w2cat.shape),
            _const_spec(b2cat.shape),
            _const_spec((1, d)),
            _const_spec((1, d)),
        ],
        out_specs=(wide_spec,) * 6 + (half_spec, half_spec, pl.BlockSpec((1, tm, 2 * kd), row)),
        scratch_shapes=[pltpu.VMEM((tm, d), BF16)],
        compiler_params=_params("parallel", "parallel"),
        name="in_proj",
    )(x, shift, scale, norm_g, w_all, w2cat, b2cat, ln_g, ln_b)


def _gla_kernel(qf_ref, kf_ref, laf_ref, vf_ref, qb_ref, kb_ref, lab_ref, vb_ref, s0f_ref, s0b_ref,
                of_ref, ob_ref, sf_ref, sb_ref, *, n_chunks, hk, hv):
    @pl.when(pl.program_id(1) == 0)
    def _():
        sf_ref[...] = s0f_ref[...]
        sb_ref[...] = s0b_ref[...]

    r = lax.broadcasted_iota(jnp.int32, (GLA_CHUNK, GLA_CHUNK), 0)
    c = lax.broadcasted_iota(jnp.int32, (GLA_CHUNK, GLA_CHUNK), 1)
    fwd = (qf_ref, kf_ref, laf_ref, vf_ref, of_ref, sf_ref, c <= r, False)
    bwd = (qb_ref, kb_ref, lab_ref, vb_ref, ob_ref, sb_ref, c >= r, True)

    def decayed_operands(refs, ci):
        q_ref, k_ref, b_ref, _, _, _, _, reverse = refs
        rows = slice(ci * GLA_CHUNK, (ci + 1) * GLA_CHUNK)
        b = b_ref[0, rows, :]
        b_last = b[0:1] if reverse else b[GLA_CHUNK - 1:GLA_CHUNK]
        k = k_ref[0, rows, :]
        qd = (q_ref[0, rows, :] * jnp.exp(b)).astype(BF16)
        ki = (k * jnp.exp(-b)).astype(BF16)
        kdt = (k * jnp.exp(b_last - b)).T.astype(BF16)
        dec = jnp.broadcast_to(jnp.exp(b_last), (SUBLANE, b.shape[1])).T
        return qd, ki, kdt, dec

    def chunks_of(step):
        return ((fwd, step), (bwd, n_chunks - 1 - step))

    ops = [decayed_operands(refs, ci) for refs, ci in chunks_of(0)]
    for step in range(n_chunks):
        streams = [(refs, ci, h, op) for (refs, ci), op in zip(chunks_of(step), ops) for h in range(GLA_HEADS)]
        raw, kvs = [], []
        for (_, _, _, v_ref, _, _, _, _), ci, h, (qd, ki, kdt, _) in streams:
            rows = slice(ci * GLA_CHUNK, (ci + 1) * GLA_CHUNK)
            ks = slice(h * hk, (h + 1) * hk)
            raw.append(_dot_nt(qd[:, ks], ki[:, ks]))
            kvs.append(_dot(kdt[ks, :], v_ref[0, rows, h * hv:(h + 1) * hv]))
        if step + 1 < n_chunks:
            ops = [decayed_operands(refs, ci) for refs, ci in chunks_of(step + 1)]
        for ((_, _, _, v_ref, o_ref, st_ref, keep, _), ci, h, (qd, _, _, dec)), sc, kv in zip(streams, raw, kvs):
            rows = slice(ci * GLA_CHUNK, (ci + 1) * GLA_CHUNK)
            ks = slice(h * hk, (h + 1) * hk)
            vs = slice(h * hv, (h + 1) * hv)
            scores = jnp.where(keep, sc, 0.0).astype(BF16)
            state = st_ref[0, h]
            o_inter = _dot(qd[:, ks], state.astype(BF16))
            o_ref[0, rows, vs] = (_dot(scores, v_ref[0, rows, vs]) + o_inter).astype(BF16)
            st_ref[0, h] = state * jnp.broadcast_to(dec[ks, 0:1], (hk, hv)) + kv


def _gla(q, k, v, la, s0f, s0b, *, tile):
    bsz, length, kd = q.shape
    vd = v.shape[-1]
    hk, hv = kd // GLA_HEADS, vd // GLA_HEADS
    nt = length // tile
    fwd = lambda b, i: (b, i, 0)
    bwd = lambda b, i: (b, nt - 1 - i, 0)
    st_spec = pl.BlockSpec((1, GLA_HEADS, hk, hv), lambda b, i: (b, 0, 0, 0))
    out = jax.ShapeDtypeStruct((bsz, length, vd), BF16)

    def direction_specs(idx, la_col):
        return [
            pl.BlockSpec((1, tile, kd), idx),
            pl.BlockSpec((1, tile, kd), idx),
            pl.BlockSpec((1, tile, kd), lambda b, i: (b, idx(b, i)[1], la_col)),
            pl.BlockSpec((1, tile, vd), idx),
        ]

    return pl.pallas_call(
        functools.partial(_gla_kernel, n_chunks=tile // GLA_CHUNK, hk=hk, hv=hv),
        out_shape=(out, out, jax.ShapeDtypeStruct(s0f.shape, F32), jax.ShapeDtypeStruct(s0b.shape, F32)),
        grid=(bsz, nt),
        in_specs=direction_specs(fwd, 0) + direction_specs(bwd, 1) + [st_spec, st_spec],
        out_specs=(pl.BlockSpec((1, tile, vd), fwd), pl.BlockSpec((1, tile, vd), bwd), st_spec, st_spec),
        compiler_params=_params("parallel", "arbitrary"),
        name="gla_scan",
    )(q, k, la, v, q, k, la, v, s0f, s0b)


def _mix_kernel(x_ref, gu_ref, vn_ref, of_ref, ob_ref, rs_ref, sga_ref, sgb_ref, ws_ref, bs_ref, gn_ref,
                wa_ref, wb_ref, wo_ref, g1_ref, n2_ref, sh2_ref, sc2_ref, x1_ref, h2_ref,
                s_buf, a_buf, b_buf, m_buf, *, tm, hv):
    d = x_ref.shape[-1]
    for n in range(tm // SGU_CHUNK):
        rows = slice(n * SGU_CHUNK, (n + 1) * SGU_CHUNK)
        for g in range(SGU_GROUPS):
            cols = slice(g * LANE, (g + 1) * LANE)
            s_buf[rows, cols] = _dot(ws_ref[g], vn_ref[0, rows, cols]) + bs_ref[:, cols]
    for h in range(GLA_HEADS):
        cols = slice(h * hv, (h + 1) * hv)
        oh = of_ref[0, :, cols].astype(F32) + ob_ref[0, :, cols].astype(F32)
        ms = jnp.mean(oh * oh, axis=-1, keepdims=True)
        on = oh * lax.rsqrt(ms + EPS) * gn_ref[:, cols]
        b_buf[:, cols] = (on * rs_ref[0, :, cols].astype(F32)).astype(BF16)
    proj_b = _dot(b_buf[...], wb_ref[:, :d])
    a_buf[...] = (gu_ref[0].astype(F32) * s_buf[...]).astype(BF16)
    proj_a = _dot(a_buf[...], wa_ref[:, :d])
    m_buf[...] = (sga_ref[0].astype(F32) * proj_a + sgb_ref[0].astype(F32) * proj_b).astype(BF16)
    x1 = x_ref[0] + g1_ref[0] * _dot(m_buf[...], wo_ref[:, :d])
    x1_ref[0] = x1
    h2_ref[0] = _rms_mod(x1, n2_ref[...], sh2_ref[0], sc2_ref[0]).astype(BF16)


def _mix(x, gu, vn, o_f, o_b, rs, sga, sgb, ws, bs_full, gn_g, wa, wb, wo, g1, n2_g, sh2, sc2, *, tm):
    bsz, length, d = x.shape
    row = lambda b, i: (b, i, 0)
    mod = lambda b, i: (b, 0, 0)
    tile = pl.BlockSpec((1, tm, d), row)
    modv = pl.BlockSpec((1, 1, d), mod)
    return pl.pallas_call(
        functools.partial(_mix_kernel, tm=tm, hv=d // GLA_HEADS),
        out_shape=(jax.ShapeDtypeStruct((bsz, length, d), F32), jax.ShapeDtypeStruct((bsz, length, d), BF16)),
        grid=(bsz, length // tm),
        in_specs=[tile] * 8 + [
            _const_spec(ws.shape),
            _const_spec(bs_full.shape),
            _const_spec((1, d)),
            _const_spec(wa.shape),
            _const_spec(wb.shape),
            _const_spec(wo.shape),
            modv,
            _const_spec((1, d)),
            modv,
            modv,
        ],
        out_specs=(tile, tile),
        scratch_shapes=[pltpu.VMEM((tm, d), F32)] + [pltpu.VMEM((tm, d), BF16) for _ in range(3)],
        compiler_params=_params("parallel", "parallel"),
        name="mix",
    )(x, gu, vn, o_f, o_b, rs, sga, sgb, ws, bs_full, gn_g, wa, wb, wo, g1, n2_g, sh2, sc2)


def _ffn_kernel(h_ref, x_ref, wu_ref, cw_ref, cb_ref, wd_ref, g2_ref, fg_ref, o_ref,
                a0_ref, a1_ref, val0_ref, val1_ref, above_ref, *gated_refs,
                n_tiles, tile, img_w, final_norm):
    s = pl.program_id(1)
    dff = a0_ref.shape[-1]
    d = x_ref.shape[-1]
    blk = gated_refs[0].shape[0]
    rows_img = tile // img_w
    has_rows = n_tiles * rows_img > 1
    bufs = ((a0_ref, val0_ref), (a1_ref, val1_ref))

    sub = lax.broadcasted_iota(jnp.int32, (SUBLANE, LANE), 0)
    not_first = sub >= 1
    not_last = sub <= SUBLANE - 2

    def up_pieces(a_dst, val_dst):
        def piece(rows, col, dst, cast):
            def run():
                res = _dot(h_ref[0, rows, :], wu_ref[:, col:col + UP_SLAB])
                dst[rows, col % dff:col % dff + UP_SLAB] = res.astype(BF16) if cast else res
            return run
        pieces = []
        for j in range(tile // blk):
            rows = slice(j * blk, (j + 1) * blk)
            pieces += [piece(rows, col, a_dst, False) for col in range(0, dff, UP_SLAB)]
            pieces += [piece(rows, dff + col, val_dst, True) for col in range(0, dff, UP_SLAB)]
        return pieces

    def conv_strip(cur, val, below, r, c, gated, out_rows):
        tok = slice(r * img_w, (r + 1) * img_w)
        cols = slice(c * LANE, (c + 1) * LANE)
        srcs = [(1, cur, tok)]
        if has_rows:
            srcs.append((0, cur, slice((r - 1) * img_w, r * img_w)) if r > 0 else (0, above_ref, slice(0, img_w)))
            if r < rows_img - 1:
                srcs.append((2, cur, slice((r + 1) * img_w, (r + 2) * img_w)))
            elif below is not None:
                srcs.append((2, below, slice(0, img_w)))
        acc = [None, None, None]
        for kr, ref, rws in srcs:
            src = ref[rws, cols]
            for dc in (-1, 0, 1):
                tap = cw_ref[kr * CONV_K + dc + 1:kr * CONV_K + dc + 2, cols]
                acc[dc + 1] = tap * src if acc[dc + 1] is None else acc[dc + 1] + tap * src
        n_sl = img_w // SUBLANE
        dn = [pltpu.roll(acc[0][j * SUBLANE:(j + 1) * SUBLANE], 1, 0) for j in range(n_sl)]
        upw = [pltpu.roll(acc[2][j * SUBLANE:(j + 1) * SUBLANE], SUBLANE - 1, 0) for j in range(n_sl)]
        left = jnp.concatenate(
            [jnp.where(not_first, dn[j], dn[j - 1] if j > 0 else 0.0) for j in range(n_sl)], axis=0)
        right = jnp.concatenate(
            [jnp.where(not_last, upw[j], upw[j + 1] if j < n_sl - 1 else 0.0) for j in range(n_sl)], axis=0)
        conv = acc[1] + left + right + cb_ref[:, cols]
        gated[out_rows, cols] = (_gelu(conv) * val[tok, cols].astype(F32)).astype(BF16)

    def conv_down(cur, val, below, pieces=()):
        slabs = dff // UP_SLAB
        total = len(gated_refs) * slabs
        assert not pieces or len(pieces) == 2 * total
        done = issued = 0
        for j, gated in enumerate(gated_refs):
            y = None
            for k in range(slabs):
                done += 1
                while issued * total < done * len(pieces):
                    pieces[issued]()
                    issued += 1
                for rr in range(blk // img_w):
                    for c in range(k * (UP_SLAB // LANE), (k + 1) * (UP_SLAB // LANE)):
                        conv_strip(cur, val, below, j * (blk // img_w) + rr, c, gated,
                                   slice(rr * img_w, (rr + 1) * img_w))
                ks = slice(k * UP_SLAB, (k + 1) * UP_SLAB)
                part = _dot(gated[:, ks], wd_ref[ks, :d])
                y = part if y is None else y + part
            rows = slice(j * blk, (j + 1) * blk)
            x2 = x_ref[0, rows, :] + g2_ref[0] * y
            if final_norm:
                ms = jnp.mean(x2 * x2, axis=-1, keepdims=True)
                x2 = x2 * lax.rsqrt(ms + EPS) * fg_ref[...]
            o_ref[0, rows, :] = x2
        if has_rows:
            above_ref[...] = cur[tile - img_w:tile, :]

    @pl.when(s == 0)
    def _():
        if has_rows:
            above_ref[...] = jnp.zeros(above_ref.shape, F32)
        for run in up_pieces(*bufs[0]):
            run()

    for parity in sorted({step % 2 for step in range(1, n_tiles)}):
        @pl.when(jnp.logical_and(jnp.logical_and(s > 0, s < n_tiles), lax.rem(s, 2) == parity))
        def _():
            new, (cur, val) = bufs[parity], bufs[1 - parity]
            conv_down(cur, val, new[0], up_pieces(*new))

    @pl.when(s == n_tiles)
    def _():
        conv_down(*bufs[(n_tiles - 1) % 2], None)


def _ffn(h2, x1, w_up, conv_w, conv_b, w_down, g2, final_g, *, tile, img_w, final_norm):
    bsz, length, d = x1.shape
    dff = w_down.shape[0]
    n_tiles = length // tile
    assert n_tiles > 1 or tile == img_w, "a single tile must be a single image row"
    blk = max(FFN_DOWN_BLOCK, img_w)
    up_tile = lambda b, s: (b, jnp.minimum(s, n_tiles - 1), 0)
    down_tile = lambda b, s: (b, jnp.maximum(s - 1, 0), 0)
    return pl.pallas_call(
        functools.partial(_ffn_kernel, n_tiles=n_tiles, tile=tile, img_w=img_w, final_norm=final_norm),
        out_shape=jax.ShapeDtypeStruct(x1.shape, F32),
        grid=(bsz, n_tiles + 1),
        in_specs=[
            pl.BlockSpec((1, tile, d), up_tile),
            pl.BlockSpec((1, tile, d), down_tile),
            _const_spec(w_up.shape),
            _const_spec(conv_w.shape),
            _const_spec(conv_b.shape),
            _const_spec(w_down.shape),
            pl.BlockSpec((1, 1, d), lambda b, s: (b, 0, 0)),
            _const_spec((1, d)),
        ],
        out_specs=pl.BlockSpec((1, tile, d), down_tile),
        scratch_shapes=[pltpu.VMEM((tile, dff), F32), pltpu.VMEM((tile, dff), F32),
                        pltpu.VMEM((tile, dff), BF16), pltpu.VMEM((tile, dff), BF16),
                        pltpu.VMEM((img_w, dff), F32)]
        + [pltpu.VMEM((blk, dff), BF16) for _ in range(tile // blk)],
        compiler_params=_params("parallel", "arbitrary"),
        name="ffn",
    )(h2, x1, w_up, conv_w, conv_b, w_down, g2, final_g)


def _stream_layer(x, mods, wts, s0f, s0b, *, tm, gla_tile, ffn_tile, img_w, states_only, final_norm):
    sh1, sc1, g1, sh2, sc2, g2 = mods
    gu, vn, v, rs, sga, sgb, q, k, la = _in_proj(
        x, sh1, sc1, wts["norm1_g"], wts["w_all"], wts["w2cat"], wts["b2cat"],
        wts["sgu_ln_g"], wts["sgu_ln_b"], tm=tm)
    o_f, o_b, s_f, s_b = _gla(q, k, v, la, s0f, s0b, tile=gla_tile)
    if states_only:
        return None, s_f, s_b
    x1, h2 = _mix(x, gu, vn, o_f, o_b, rs, sga, sgb, wts["sgu_w"], wts["sgu_b_full"], wts["gla_norm_g"],
                  wts["w_br_a"], wts["w_br_b"], wts["w_o"], g1, wts["norm2_g"], sh2, sc2, tm=tm)
    x2 = _ffn(h2, x1, wts["ffn_w_up"], wts["ffn_conv_w"], wts["ffn_conv_b"], wts["ffn_w_down"], g2, wts["final_g"],
              tile=ffn_tile, img_w=img_w, final_norm=final_norm)
    return x2, s_f, s_b


def kernel(x, c, ctx, c_ctx, w_ada, b_ada, norm1_g, norm2_g, w_in, sgu_ln_g, sgu_ln_b, sgu_w, sgu_b, gla_w2, gla_b2, gla_norm_g, w_br_a, w_br_b, w_o, ffn_w_up, ffn_conv_w, ffn_conv_b, ffn_w_down, final_norm_g):
    bsz, seq, d = x.shape
    ctx_len = ctx.shape[1]
    depth = w_in.shape[0]
    kd = d // 2
    hk, hv = kd // GLA_HEADS, d // GLA_HEADS
    dff = ffn_w_down.shape[1]

    n_rows = -(-(bsz + 1) // SUBLANE) * SUBLANE
    cc = jnp.zeros((n_rows, d), F32).at[:bsz].set(c).at[bsz].set(c_ctx)
    mods = _ada_mods(cc, w_ada, b_ada)

    w_all = _bf16_odd_tiles(w_in)
    w2cat = jnp.zeros((depth, LANE, 2 * kd), F32)
    w2cat = w2cat.at[:, :GLA_RANK, :kd].set(gla_w2[:, 0]).at[:, GLA_RANK:2 * GLA_RANK, kd:].set(gla_w2[:, 1])
    w2cat = w2cat.astype(BF16)
    b2cat = gla_b2.reshape(depth, 1, 2 * kd)
    sgu_b_full = jnp.repeat(jnp.swapaxes(sgu_b, 1, 2), LANE, axis=2)

    zero_state = jnp.zeros((bsz, GLA_HEADS, hk, hv), F32)
    tm = 512
    for l in range(depth):
        last = l == depth - 1
        wts = {
            "norm1_g": norm1_g[l][None], "norm2_g": norm2_g[l][None],
            "w_all": w_all[l], "w2cat": w2cat[l], "b2cat": b2cat[l],
            "sgu_ln_g": sgu_ln_g[l][None], "sgu_ln_b": sgu_ln_b[l][None],
            "sgu_w": sgu_w[l].astype(BF16), "sgu_b_full": sgu_b_full[l],
            "gla_norm_g": gla_norm_g[l][None],
            "w_br_a": _bf16_odd_tiles(w_br_a[l]), "w_br_b": _bf16_odd_tiles(w_br_b[l]),
            "w_o": _bf16_odd_tiles(w_o[l]),
            "ffn_w_up": ffn_w_up[l].astype(BF16), "ffn_conv_w": ffn_conv_w[l].reshape(CONV_K * CONV_K, dff),
            "ffn_conv_b": ffn_conv_b[l][None], "ffn_w_down": _bf16_odd_tiles(ffn_w_down[l]),
            "final_g": final_norm_g[None],
        }
        m = mods[l].reshape(n_rows, N_MOD, d)
        mods_x = [m[:bsz, j][:, None, :] for j in range(N_MOD)]
        mods_c = [jnp.broadcast_to(m[bsz, j][None, None, :], (bsz, 1, d)) for j in range(N_MOD)]
        ctx, s_f, s_b = _stream_layer(
            ctx, mods_c, wts, zero_state, zero_state, tm=min(tm, ctx_len), gla_tile=ctx_len, ffn_tile=ctx_len,
            img_w=ctx_len, states_only=last, final_norm=False)
        x, _, _ = _stream_layer(
            x, mods_x, wts, s_f, s_b, tm=tm, gla_tile=1024, ffn_tile=128, img_w=GRID_W,
            states_only=False, final_norm=last)
    return x
```

```python
import functools

import jax
import jax.numpy as jnp
from jax import lax
from jax.experimental import pallas as pl
from jax.experimental.pallas import tpu as pltpu

F32 = jnp.float32
BF16 = jnp.bfloat16

EPS = 1e-6
N_MOD = 6
SGU_CHUNK = 128
SGU_GROUPS = 8
GLA_HEADS = 4
GLA_RANK = 16
GLA_GATE_NORMALIZER = 16.0
GLA_CHUNK = 64
GRID_W = 64
CONV_K = 3
LANE = 128
SUBLANE = 8
VMEM_LIMIT_BYTES = 56 * 1024 * 1024
INV_SQRT2 = 0.7071067811865476
FFN_DOWN_BLOCK = 128
UP_SLAB = 256


def _params(*semantics):
    return pltpu.CompilerParams(dimension_semantics=semantics, vmem_limit_bytes=VMEM_LIMIT_BYTES)


def _const_spec(shape):
    zeros = (0,) * len(shape)
    return pl.BlockSpec(shape, lambda *_: zeros, pipeline_mode=pl.Buffered(1))


def _gelu(v):
    return 0.5 * v * (1.0 + lax.erf(v * INV_SQRT2))


def _sigmoid(v):
    return 1.0 / (1.0 + jnp.exp(-v))


def _log_sigmoid(v):
    return jnp.minimum(v, 0.0) - jnp.log(1.0 + jnp.exp(-jnp.abs(v)))


def _rms_mod(x, g, shift, scale):
    ms = jnp.mean(x * x, axis=-1, keepdims=True)
    y = x * lax.rsqrt(ms + EPS) * g
    return y * (1.0 + scale) + shift


def _dot(a, b):
    return jnp.dot(a, b, preferred_element_type=F32)


def _dot_nt(a, b):
    return lax.dot_general(a, b, (((1,), (1,)), ((), ())), preferred_element_type=F32)


def _bf16_odd_tiles(w):
    tiles = -(-w.shape[-1] // LANE)
    tiles += 1 - tiles % 2
    pad = [(0, 0)] * (w.ndim - 1) + [(0, tiles * LANE - w.shape[-1])]
    return jnp.pad(w.astype(BF16), pad)


def _ada_kernel(c_ref, w_ref, b_ref, o_ref):
    c = c_ref[...]
    s = (c * _sigmoid(c)).astype(BF16)
    o_ref[0] = _dot(s, w_ref[0].astype(BF16)) + b_ref[0]


def _ada_mods(cc, w_ada, b_ada):
    depth, d, n = w_ada.shape
    tn = n // 4
    return pl.pallas_call(
        _ada_kernel,
        out_shape=jax.ShapeDtypeStruct((depth, cc.shape[0], n), F32),
        grid=(depth, n // tn),
        in_specs=[
            pl.BlockSpec(cc.shape, lambda l, j: (0, 0)),
            pl.BlockSpec((1, d, tn), lambda l, j: (l, 0, j)),
            pl.BlockSpec((1, 1, tn), lambda l, j: (l, 0, j)),
        ],
        out_specs=pl.BlockSpec((1, cc.shape[0], tn), lambda l, j: (l, 0, j)),
        compiler_params=_params("parallel", "parallel"),
        name="ada_mods",
    )(cc, w_ada, b_ada.reshape(depth, 1, n))


def _chunk_cumsum(x, reverse):
    n_sl = x.shape[0] // SUBLANE
    sub = lax.broadcasted_iota(jnp.int32, (SUBLANE, x.shape[1]), 0)
    tiles = []
    for j in range(n_sl):
        y = x[j * SUBLANE:(j + 1) * SUBLANE]
        for s in (1, 2, 4):
            if reverse:
                y = y + jnp.where(sub < SUBLANE - s, pltpu.roll(y, SUBLANE - s, 0), 0.0)
            else:
                y = y + jnp.where(sub >= s, pltpu.roll(y, s, 0), 0.0)
        tiles.append(y)
    edge = 0 if reverse else SUBLANE - 1
    carry = None
    for j in (range(n_sl - 1, -1, -1) if reverse else range(n_sl)):
        if carry is not None:
            tiles[j] = tiles[j] + carry
        carry = jnp.broadcast_to(tiles[j][edge:edge + 1], tiles[j].shape)
    return jnp.concatenate(tiles, axis=0)


def _inproj_kernel(x_ref, sh_ref, sc_ref, g_ref, w_ref, w2_ref, b2_ref, lng_ref, lnb_ref,
                   gu_ref, vn_ref, v_ref, rs_ref, sga_ref, sgb_ref, q_ref, k_ref, la_ref, hb_ref,
                   *, d, kd, q_scale):
    hb_ref[...] = _rms_mod(x_ref[0], g_ref[...], sh_ref[0], sc_ref[0]).astype(BF16)

    def seg(lo, width):
        return _dot(hb_ref[...], w_ref[:, lo:lo + width])

    off_u, off_vs, off_q, off_r = 0, d, 2 * d, 2 * d + kd
    off_ga, off_gb, off_k, off_v = 3 * d + kd, 4 * d + kd, 5 * d + kd, 5 * d + 2 * kd
    off_lr = 6 * d + 2 * kd

    lr = seg(off_lr, LANE)
    q = seg(off_q, kd)
    lr = lr.astype(BF16)
    k = seg(off_k, kd)
    q_ref[0] = q * q_scale
    z = _dot(lr, w2_ref[...]) + b2_ref[...]
    k_ref[0] = k
    ga = seg(off_ga, d)
    la = _log_sigmoid(z) * (1.0 / GLA_GATE_NORMALIZER)
    for ci in range(x_ref.shape[1] // GLA_CHUNK):
        rows = slice(ci * GLA_CHUNK, (ci + 1) * GLA_CHUNK)
        la_ref[0, rows, :kd] = _chunk_cumsum(la[rows, :kd], False)
        la_ref[0, rows, kd:] = _chunk_cumsum(la[rows, kd:], True)
    gb = seg(off_gb, d)
    sga_ref[0] = _sigmoid(ga).astype(BF16)
    r = seg(off_r, d)
    sgb_ref[0] = _sigmoid(gb).astype(BF16)
    u = seg(off_u, d)
    rs_ref[0] = (r * _sigmoid(r)).astype(BF16)
    vs = seg(off_vs, d)
    gu_ref[0] = _gelu(u).astype(BF16)
    v = seg(off_v, d)
    gv = _gelu(vs)
    mu = jnp.mean(gv, axis=-1, keepdims=True)
    cen = gv - mu
    var = jnp.mean(cen * cen, axis=-1, keepdims=True)
    vn_ref[0] = (cen * lax.rsqrt(var + EPS) * lng_ref[...] + lnb_ref[...]).astype(BF16)
    v_ref[0] = v.astype(BF16)


def _in_proj(x, shift, scale, norm_g, w_all, w2cat, b2cat, ln_g, ln_b, *, tm):
    bsz, length, d = x.shape
    kd = d // 2
    hk = kd // GLA_HEADS
    row = lambda b, i: (b, i, 0)
    mod = lambda b, i: (b, 0, 0)
    wide = jax.ShapeDtypeStruct((bsz, length, d), BF16)
    half = jax.ShapeDtypeStruct((bsz, length, kd), F32)
    wide_spec = pl.BlockSpec((1, tm, d), row)
    half_spec = pl.BlockSpec((1, tm, kd), row)
    return pl.pallas_call(
        functools.partial(_inproj_kernel, d=d, kd=kd, q_scale=float(hk) ** -0.5),
        out_shape=(wide,) * 6 + (half, half, jax.ShapeDtypeStruct((bsz, length, 2 * kd), F32)),
        grid=(bsz, length // tm),
        in_specs=[
            pl.BlockSpec((1, tm, d), row),
            pl.BlockSpec((1, 1, d), mod),
            pl.BlockSpec((1, 1, d), mod),
            _const_spec((1, d)),
            _const_spec(w_all.shape),
            _const_spec(w2cat.shape),
            _const_spec(b2cat.shape),
            _const_spec((1, d)),
            _const_spec((1, d)),
        ],
        out_specs=(wide_spec,) * 6 + (half_spec, half_spec, pl.BlockSpec((1, tm, 2 * kd), row)),
        scratch_shapes=[pltpu.VMEM((tm, d), BF16)],
        compiler_params=_params("parallel", "parallel"),
        name="in_proj",
    )(x, shift, scale, norm_g, w_all, w2cat, b2cat, ln_g, ln_b)


def _gla_kernel(qf_ref, kf_ref, laf_ref, vf_ref, qb_ref, kb_ref, lab_ref, vb_ref, s0f_ref, s0b_ref,
                of_ref, ob_ref, sf_ref, sb_ref, *, n_chunks, hk, hv):
    @pl.when(pl.program_id(1) == 0)
    def _():
        sf_ref[...] = s0f_ref[...]
        sb_ref[...] = s0b_ref[...]

    r = lax.broadcasted_iota(jnp.int32, (GLA_CHUNK, GLA_CHUNK), 0)
    c = lax.broadcasted_iota(jnp.int32, (GLA_CHUNK, GLA_CHUNK), 1)
    fwd = (qf_ref, kf_ref, laf_ref, vf_ref, of_ref, sf_ref, c <= r, False)
    bwd = (qb_ref, kb_ref, lab_ref, vb_ref, ob_ref, sb_ref, c >= r, True)

    def decayed_operands(refs, ci):
        q_ref, k_ref, b_ref, _, _, _, _, reverse = refs
        rows = slice(ci * GLA_CHUNK, (ci + 1) * GLA_CHUNK)
        b = b_ref[0, rows, :]
        b_last = b[0:1] if reverse else b[GLA_CHUNK - 1:GLA_CHUNK]
        k = k_ref[0, rows, :]
        qd = (q_ref[0, rows, :] * jnp.exp(b)).astype(BF16)
        ki = (k * jnp.exp(-b)).astype(BF16)
        kdt = (k * jnp.exp(b_last - b)).T.astype(BF16)
        dec = jnp.broadcast_to(jnp.exp(b_last), (SUBLANE, b.shape[1])).T
        return qd, ki, kdt, dec

    def chunks_of(step):
        return ((fwd, step), (bwd, n_chunks - 1 - step))

    ops = [decayed_operands(refs, ci) for refs, ci in chunks_of(0)]
    for step in range(n_chunks):
        streams = [(refs, ci, h, op) for (refs, ci), op in zip(chunks_of(step), ops) for h in range(GLA_HEADS)]
        raw, kvs = [], []
        for (_, _, _, v_ref, _, _, _, _), ci, h, (qd, ki, kdt, _) in streams:
            rows = slice(ci * GLA_CHUNK, (ci + 1) * GLA_CHUNK)
            ks = slice(h * hk, (h + 1) * hk)
            raw.append(_dot_nt(qd[:, ks], ki[:, ks]))
            kvs.append(_dot(kdt[ks, :], v_ref[0, rows, h * hv:(h + 1) * hv]))
        if step + 1 < n_chunks:
            ops = [decayed_operands(refs, ci) for refs, ci in chunks_of(step + 1)]
        for ((_, _, _, v_ref, o_ref, st_ref, keep, _), ci, h, (qd, _, _, dec)), sc, kv in zip(streams, raw, kvs):
            rows = slice(ci * GLA_CHUNK, (ci + 1) * GLA_CHUNK)
            ks = slice(h * hk, (h + 1) * hk)
            vs = slice(h * hv, (h + 1) * hv)
            scores = jnp.where(keep, sc, 0.0).astype(BF16)
            state = st_ref[0, h]
            o_inter = _dot(qd[:, ks], state.astype(BF16))
            o_ref[0, rows, vs] = (_dot(scores, v_ref[0, rows, vs]) + o_inter).astype(BF16)
            st_ref[0, h] = state * jnp.broadcast_to(dec[ks, 0:1], (hk, hv)) + kv


def _gla(q, k, v, la, s0f, s0b, *, tile):
    bsz, length, kd = q.shape
    vd = v.shape[-1]
    hk, hv = kd // GLA_HEADS, vd // GLA_HEADS
    nt = length // tile
    fwd = lambda b, i: (b, i, 0)
    bwd = lambda b, i: (b, nt - 1 - i, 0)
    st_spec = pl.BlockSpec((1, GLA_HEADS, hk, hv), lambda b, i: (b, 0, 0, 0))
    out = jax.ShapeDtypeStruct((bsz, length, vd), BF16)

    def direction_specs(idx, la_col):
        return [
            pl.BlockSpec((1, tile, kd), idx),
            pl.BlockSpec((1, tile, kd), idx),
            pl.BlockSpec((1, tile, kd), lambda b, i: (b, idx(b, i)[1], la_col)),
            pl.BlockSpec((1, tile, vd), idx),
        ]

    return pl.pallas_call(
        functools.partial(_gla_kernel, n_chunks=tile // GLA_CHUNK, hk=hk, hv=hv),
        out_shape=(out, out, jax.ShapeDtypeStruct(s0f.shape, F32), jax.ShapeDtypeStruct(s0b.shape, F32)),
        grid=(bsz, nt),
        in_specs=direction_specs(fwd, 0) + direction_specs(bwd, 1) + [st_spec, st_spec],
        out_specs=(pl.BlockSpec((1, tile, vd), fwd), pl.BlockSpec((1, tile, vd), bwd), st_spec, st_spec),
        compiler_params=_params("parallel", "arbitrary"),
        name="gla_scan",
    )(q, k, la, v, q, k, la, v, s0f, s0b)


def _mix_kernel(x_ref, gu_ref, vn_ref, of_ref, ob_ref, rs_ref, sga_ref, sgb_ref, ws_ref, bs_ref, gn_ref,
                wa_ref, wb_ref, wo_ref, g1_ref, n2_ref, sh2_ref, sc2_ref, x1_ref, h2_ref,
                s_buf, a_buf, b_buf, m_buf, *, tm, hv):
    d = x_ref.shape[-1]
    for n in range(tm // SGU_CHUNK):
        rows = slice(n * SGU_CHUNK, (n + 1) * SGU_CHUNK)
        for g in range(SGU_GROUPS):
            cols = slice(g * LANE, (g + 1) * LANE)
            s_buf[rows, cols] = _dot(ws_ref[g], vn_ref[0, rows, cols]) + bs_ref[:, cols]
    for h in range(GLA_HEADS):
        cols = slice(h * hv, (h + 1) * hv)
        oh = of_ref[0, :, cols].astype(F32) + ob_ref[0, :, cols].astype(F32)
        ms = jnp.mean(oh * oh, axis=-1, keepdims=True)
        on = oh * lax.rsqrt(ms + EPS) * gn_ref[:, cols]
        b_buf[:, cols] = (on * rs_ref[0, :, cols].astype(F32)).astype(BF16)
    proj_b = _dot(b_buf[...], wb_ref[:, :d])
    a_buf[...] = (gu_ref[0].astype(F32) * s_buf[...]).astype(BF16)
    proj_a = _dot(a_buf[...], wa_ref[:, :d])
    m_buf[...] = (sga_ref[0].astype(F32) * proj_a + sgb_ref[0].astype(F32) * proj_b).astype(BF16)
    x1 = x_ref[0] + g1_ref[0] * _dot(m_buf[...], wo_ref[:, :d])
    x1_ref[0] = x1
    h2_ref[0] = _rms_mod(x1, n2_ref[...], sh2_ref[0], sc2_ref[0]).astype(BF16)


def _mix(x, gu, vn, o_f, o_b, rs, sga, sgb, ws, bs_full, gn_g, wa, wb, wo, g1, n2_g, sh2, sc2, *, tm):
    bsz, length, d = x.shape
    row = lambda b, i: (b, i, 0)
    mod = lambda b, i: (b, 0, 0)
    tile = pl.BlockSpec((1, tm, d), row)
    modv = pl.BlockSpec((1, 1, d), mod)
    return pl.pallas_call(
        functools.partial(_mix_kernel, tm=tm, hv=d // GLA_HEADS),
        out_shape=(jax.ShapeDtypeStruct((bsz, length, d), F32), jax.ShapeDtypeStruct((bsz, length, d), BF16)),
        grid=(bsz, length // tm),
        in_specs=[tile] * 8 + [
            _const_spec(ws.shape),
            _const_spec(bs_full.shape),
            _const_spec((1, d)),
            _const_spec(wa.shape),
            _const_spec(wb.shape),
            _const_spec(wo.shape),
            modv,
            _const_spec((1, d)),
            modv,
            modv,
        ],
        out_specs=(tile, tile),
        scratch_shapes=[pltpu.VMEM((tm, d), F32)] + [pltpu.VMEM((tm, d), BF16) for _ in range(3)],
        compiler_params=_params("parallel", "parallel"),
        name="mix",
    )(x, gu, vn, o_f, o_b, rs, sga, sgb, ws, bs_full, gn_g, wa, wb, wo, g1, n2_g, sh2, sc2)


def _ffn_kernel(h_ref, x_ref, wu_ref, cw_ref, cb_ref, wd_ref, g2_ref, fg_ref, o_ref,
                a0_ref, a1_ref, val0_ref, val1_ref, above_ref, *gated_refs,
                n_tiles, tile, img_w, final_norm):
    s = pl.program_id(1)
    dff = a0_ref.shape[-1]
    d = x_ref.shape[-1]
    blk = gated_refs[0].shape[0]
    rows_img = tile // img_w
    has_rows = n_tiles * rows_img > 1
    bufs = ((a0_ref, val0_ref), (a1_ref, val1_ref))

    sub = lax.broadcasted_iota(jnp.int32, (SUBLANE, LANE), 0)
    not_first = sub >= 1
    not_last = sub <= SUBLANE - 2

    def up_pieces(a_dst, val_dst):
        def piece(rows, col, dst, cast):
            def run():
                res = _dot(h_ref[0, rows, :], wu_ref[:, col:col + UP_SLAB])
                dst[rows, col % dff:col % dff + UP_SLAB] = res.astype(BF16) if cast else res
            return run
        pieces = []
        for j in range(tile // blk):
            rows = slice(j * blk, (j + 1) * blk)
            pieces += [piece(rows, col, a_dst, False) for col in range(0, dff, UP_SLAB)]
            pieces += [piece(rows, dff + col, val_dst, True) for col in range(0, dff, UP_SLAB)]
        return pieces

    def conv_strip(cur, val, below, r, c, gated, out_rows):
        tok = slice(r * img_w, (r + 1) * img_w)
        cols = slice(c * LANE, (c + 1) * LANE)
        srcs = [(1, cur, tok)]
        if has_rows:
            srcs.append((0, cur, slice((r - 1) * img_w, r * img_w)) if r > 0 else (0, above_ref, slice(0, img_w)))
            if r < rows_img - 1:
                srcs.append((2, cur, slice((r + 1) * img_w, (r + 2) * img_w)))
            elif below is not None:
                srcs.append((2, below, slice(0, img_w)))
        acc = [None, None, None]
        for kr, ref, rws in srcs:
            src = ref[rws, cols]
            for dc in (-1, 0, 1):
                tap = cw_ref[kr * CONV_K + dc + 1:kr * CONV_K + dc + 2, cols]
                acc[dc + 1] = tap * src if acc[dc + 1] is None else acc[dc + 1] + tap * src
        n_sl = img_w // SUBLANE
        dn = [pltpu.roll(acc[0][j * SUBLANE:(j + 1) * SUBLANE], 1, 0) for j in range(n_sl)]
        upw = [pltpu.roll(acc[2][j * SUBLANE:(j + 1) * SUBLANE], SUBLANE - 1, 0) for j in range(n_sl)]
        left = jnp.concatenate(
            [jnp.where(not_first, dn[j], dn[j - 1] if j > 0 else 0.0) for j in range(n_sl)], axis=0)
        right = jnp.concatenate(
            [jnp.where(not_last, upw[j], upw[j + 1] if j < n_sl - 1 else 0.0) for j in range(n_sl)], axis=0)
        conv = acc[1] + left + right + cb_ref[:, cols]
        gated[out_rows, cols] = (_gelu(conv) * val[tok, cols].astype(F32)).astype(BF16)

    def conv_down(cur, val, below, pieces=()):
        slabs = dff // UP_SLAB
        total = len(gated_refs) * slabs
        assert not pieces or len(pieces) == 2 * total
        done = issued = 0
        for j, gated in enumerate(gated_refs):
            y = None
            for k in range(slabs):
                done += 1
                while issued * total < done * len(pieces):
                    pieces[issued]()
                    issued += 1
                for rr in range(blk // img_w):
                    for c in range(k * (UP_SLAB // LANE), (k + 1) * (UP_SLAB // LANE)):
                        conv_strip(cur, val, below, j * (blk // img_w) + rr, c, gated,
                                   slice(rr * img_w, (rr + 1) * img_w))
                ks = slice(k * UP_SLAB, (k + 1) * UP_SLAB)
                part = _dot(gated[:, ks], wd_ref[ks, :d])
                y = part if y is None else y + part
            rows = slice(j * blk, (j + 1) * blk)
            x2 = x_ref[0, rows, :] + g2_ref[0] * y
            if final_norm:
                ms = jnp.mean(x2 * x2, axis=-1, keepdims=True)
                x2 = x2 * lax.rsqrt(ms + EPS) * fg_ref[...]
            o_ref[0, rows, :] = x2
        if has_rows:
            above_ref[...] = cur[tile - img_w:tile, :]

    @pl.when(s == 0)
    def _():
        if has_rows:
            above_ref[...] = jnp.zeros(above_ref.shape, F32)
        for run in up_pieces(*bufs[0]):
            run()

    for parity in sorted({step % 2 for step in range(1, n_tiles)}):
        @pl.when(jnp.logical_and(jnp.logical_and(s > 0, s < n_tiles), lax.rem(s, 2) == parity))
        def _():
            new, (cur, val) = bufs[parity], bufs[1 - parity]
            conv_down(cur, val, new[0], up_pieces(*new))

    @pl.when(s == n_tiles)
    def _():
        conv_down(*bufs[(n_tiles - 1) % 2], None)


def _ffn(h2, x1, w_up, conv_w, conv_b, w_down, g2, final_g, *, tile, img_w, final_norm):
    bsz, length, d = x1.shape
    dff = w_down.shape[0]
    n_tiles = length // tile
    assert n_tiles > 1 or tile == img_w, "a single tile must be a single image row"
    blk = max(FFN_DOWN_BLOCK, img_w)
    up_tile = lambda b, s: (b, jnp.minimum(s, n_tiles - 1), 0)
    down_tile = lambda b, s: (b, jnp.maximum(s - 1, 0), 0)
    return pl.pallas_call(
        functools.partial(_ffn_kernel, n_tiles=n_tiles, tile=tile, img_w=img_w, final_norm=final_norm),
        out_shape=jax.ShapeDtypeStruct(x1.shape, F32),
        grid=(bsz, n_tiles + 1),
        in_specs=[
            pl.BlockSpec((1, tile, d), up_tile),
            pl.BlockSpec((1, tile, d), down_tile),
            _const_spec(w_up.shape),
            _const_spec(conv_w.shape),
            _const_spec(conv_b.shape),
            _const_spec(w_down.shape),
            pl.BlockSpec((1, 1, d), lambda b, s: (b, 0, 0)),
            _const_spec((1, d)),
        ],
        out_specs=pl.BlockSpec((1, tile, d), down_tile),
        scratch_shapes=[pltpu.VMEM((tile, dff), F32), pltpu.VMEM((tile, dff), F32),
                        pltpu.VMEM((tile, dff), BF16), pltpu.VMEM((tile, dff), BF16),
                        pltpu.VMEM((img_w, dff), F32)]
        + [pltpu.VMEM((blk, dff), BF16) for _ in range(tile // blk)],
        compiler_params=_params("parallel", "arbitrary"),
        name="ffn",
    )(h2, x1, w_up, conv_w, conv_b, w_down, g2, final_g)


def _stream_layer(x, mods, wts, s0f, s0b, *, tm, gla_tile, ffn_tile, img_w, states_only, final_norm):
    sh1, sc1, g1, sh2, sc2, g2 = mods
    gu, vn, v, rs, sga, sgb, q, k, la = _in_proj(
        x, sh1, sc1, wts["norm1_g"], wts["w_all"], wts["w2cat"], wts["b2cat"],
        wts["sgu_ln_g"], wts["sgu_ln_b"], tm=tm)
    o_f, o_b, s_f, s_b = _gla(q, k, v, la, s0f, s0b, tile=gla_tile)
    if states_only:
        return None, s_f, s_b
    x1, h2 = _mix(x, gu, vn, o_f, o_b, rs, sga, sgb, wts["sgu_w"], wts["sgu_b_full"], wts["gla_norm_g"],
                  wts["w_br_a"], wts["w_br_b"], wts["w_o"], g1, wts["norm2_g"], sh2, sc2, tm=tm)
    x2 = _ffn(h2, x1, wts["ffn_w_up"], wts["ffn_conv_w"], wts["ffn_conv_b"], wts["ffn_w_down"], g2, wts["final_g"],
              tile=ffn_tile, img_w=img_w, final_norm=final_norm)
    return x2, s_f, s_b


def kernel(x, c, ctx, c_ctx, w_ada, b_ada, norm1_g, norm2_g, w_in, sgu_ln_g, sgu_ln_b, sgu_w, sgu_b, gla_w2, gla_b2, gla_norm_g, w_br_a, w_br_b, w_o, ffn_w_up, ffn_conv_w, ffn_conv_b, ffn_w_down, final_norm_g):
    bsz, seq, d = x.shape
    ctx_len = ctx.shape[1]
    depth = w_in.shape[0]
    kd = d // 2
    hk, hv = kd // GLA_HEADS, d // GLA_HEADS
    dff = ffn_w_down.shape[1]

    n_rows = -(-(bsz + 1) // SUBLANE) * SUBLANE
    cc = jnp.zeros((n_rows, d), F32).at[:bsz].set(c).at[bsz].set(c_ctx)
    mods = _ada_mods(cc, w_ada, b_ada)

    w_all = _bf16_odd_tiles(w_in)
    w2cat = jnp.zeros((depth, LANE, 2 * kd), F32)
    w2cat = w2cat.at[:, :GLA_RANK, :kd].set(gla_w2[:, 0]).at[:, GLA_RANK:2 * GLA_RANK, kd:].set(gla_w2[:, 1])
    w2cat = w2cat.astype(BF16)
    b2cat = gla_b2.reshape(depth, 1, 2 * kd)
    sgu_b_full = jnp.repeat(jnp.swapaxes(sgu_b, 1, 2), LANE, axis=2)

    zero_state = jnp.zeros((bsz, GLA_HEADS, hk, hv), F32)
    tm = 512
    for l in range(depth):
        last = l == depth - 1
        wts = {
            "norm1_g": norm1_g[l][None], "norm2_g": norm2_g[l][None],
            "w_all": w_all[l], "w2cat": w2cat[l], "b2cat": b2cat[l],
            "sgu_ln_g": sgu_ln_g[l][None], "sgu_ln_b": sgu_ln_b[l][None],
            "sgu_w": sgu_w[l].astype(BF16), "sgu_b_full": sgu_b_full[l],
            "gla_norm_g": gla_norm_g[l][None],
            "w_br_a": _bf16_odd_tiles(w_br_a[l]), "w_br_b": _bf16_odd_tiles(w_br_b[l]),
            "w_o": _bf16_odd_tiles(w_o[l]),
            "ffn_w_up": ffn_w_up[l].astype(BF16), "ffn_conv_w": ffn_conv_w[l].reshape(CONV_K * CONV_K, dff),
            "ffn_conv_b": ffn_conv_b[l][None], "ffn_w_down": _bf16_odd_tiles(ffn_w_down[l]),
            "final_g": final_norm_g[None],
        }
        m = mods[l].reshape(n_rows, N_MOD, d)
        mods_x = [m[:bsz, j][:, None, :] for j in range(N_MOD)]
        mods_c = [jnp.broadcast_to(m[bsz, j][None, None, :], (bsz, 1, d)) for j in range(N_MOD)]
        ctx, s_f, s_b = _stream_layer(
            ctx, mods_c, wts, zero_state, zero_state, tm=min(tm, ctx_len), gla_tile=ctx_len, ffn_tile=ctx_len,
            img_w=ctx_len, states_only=last, final_norm=False)
        x, _, _ = _stream_layer(
            x, mods_x, wts, s_f, s_b, tm=tm, gla_tile=1024, ffn_tile=256, img_w=GRID_W,
            states_only=False, final_norm=last)
    return x
```

```python
import functools

import jax
import jax.numpy as jnp
from jax import lax
from jax.experimental import pallas as pl
from jax.experimental.pallas import tpu as pltpu

F32 = jnp.float32
BF16 = jnp.bfloat16

EPS = 1e-6
N_MOD = 6
SGU_CHUNK = 128
SGU_GROUPS = 8
GLA_HEADS = 4
GLA_RANK = 16
GLA_GATE_NORMALIZER = 16.0
GLA_CHUNK = 64
GRID_W = 64
CONV_K = 3
LANE = 128
SUBLANE = 8
VMEM_LIMIT_BYTES = 56 * 1024 * 1024
INV_SQRT2 = 0.7071067811865476
FFN_DOWN_BLOCK = 128
UP_SLAB = 256


def _params(*semantics):
    return pltpu.CompilerParams(dimension_semantics=semantics, vmem_limit_bytes=VMEM_LIMIT_BYTES)


def _const_spec(shape):
    zeros = (0,) * len(shape)
    return pl.BlockSpec(shape, lambda *_: zeros, pipeline_mode=pl.Buffered(1))


def _gelu(v):
    return 0.5 * v * (1.0 + lax.erf(v * INV_SQRT2))


def _sigmoid(v):
    return 1.0 / (1.0 + jnp.exp(-v))


def _log_sigmoid(v):
    return jnp.minimum(v, 0.0) - jnp.log(1.0 + jnp.exp(-jnp.abs(v)))


def _rms_mod(x, g, shift, scale):
    ms = jnp.mean(x * x, axis=-1, keepdims=True)
    y = x * lax.rsqrt(ms + EPS) * g
    return y * (1.0 + scale) + shift


def _dot(a, b):
    return jnp.dot(a, b, preferred_element_type=F32)


def _dot_nt(a, b):
    return lax.dot_general(a, b, (((1,), (1,)), ((), ())), preferred_element_type=F32)


def _bf16_odd_tiles(w):
    tiles = -(-w.shape[-1] // LANE)
    tiles += 1 - tiles % 2
    pad = [(0, 0)] * (w.ndim - 1) + [(0, tiles * LANE - w.shape[-1])]
    return jnp.pad(w.astype(BF16), pad)


def _ada_kernel(c_ref, w_ref, b_ref, o_ref):
    c = c_ref[...]
    s = (c * _sigmoid(c)).astype(BF16)
    o_ref[0] = _dot(s, w_ref[0].astype(BF16)) + b_ref[0]


def _ada_mods(cc, w_ada, b_ada):
    depth, d, n = w_ada.shape
    tn = n // 4
    return pl.pallas_call(
        _ada_kernel,
        out_shape=jax.ShapeDtypeStruct((depth, cc.shape[0], n), F32),
        grid=(depth, n // tn),
        in_specs=[
            pl.BlockSpec(cc.shape, lambda l, j: (0, 0)),
            pl.BlockSpec((1, d, tn), lambda l, j: (l, 0, j)),
            pl.BlockSpec((1, 1, tn), lambda l, j: (l, 0, j)),
        ],
        out_specs=pl.BlockSpec((1, cc.shape[0], tn), lambda l, j: (l, 0, j)),
        compiler_params=_params("parallel", "parallel"),
        name="ada_mods",
    )(cc, w_ada, b_ada.reshape(depth, 1, n))


def _chunk_cumsum(x, reverse):
    n_sl = x.shape[0] // SUBLANE
    sub = lax.broadcasted_iota(jnp.int32, (SUBLANE, x.shape[1]), 0)
    tiles = []
    for j in range(n_sl):
        y = x[j * SUBLANE:(j + 1) * SUBLANE]
        for s in (1, 2, 4):
            if reverse:
                y = y + jnp.where(sub < SUBLANE - s, pltpu.roll(y, SUBLANE - s, 0), 0.0)
            else:
                y = y + jnp.where(sub >= s, pltpu.roll(y, s, 0), 0.0)
        tiles.append(y)
    edge = 0 if reverse else SUBLANE - 1
    carry = None
    for j in (range(n_sl - 1, -1, -1) if reverse else range(n_sl)):
        if carry is not None:
            tiles[j] = tiles[j] + carry
        carry = jnp.broadcast_to(tiles[j][edge:edge + 1], tiles[j].shape)
    return jnp.concatenate(tiles, axis=0)


def _inproj_kernel(x_ref, sh_ref, sc_ref, g_ref, w_ref, w2_ref, b2_ref, lng_ref, lnb_ref,
                   gu_ref, vn_ref, v_ref, rs_ref, sga_ref, sgb_ref, q_ref, k_ref, la_ref, hb_ref,
                   *, d, kd, q_scale):
    hb_ref[...] = _rms_mod(x_ref[0], g_ref[...], sh_ref[0], sc_ref[0]).astype(BF16)

    def seg(lo, width):
        return _dot(hb_ref[...], w_ref[:, lo:lo + width])

    off_u, off_vs, off_q, off_r = 0, d, 2 * d, 2 * d + kd
    off_ga, off_gb, off_k, off_v = 3 * d + kd, 4 * d + kd, 5 * d + kd, 5 * d + 2 * kd
    off_lr = 6 * d + 2 * kd

    lr = seg(off_lr, LANE)
    q = seg(off_q, kd)
    lr = lr.astype(BF16)
    k = seg(off_k, kd)
    q_ref[0] = q * q_scale
    z = _dot(lr, w2_ref[...]) + b2_ref[...]
    k_ref[0] = k
    ga = seg(off_ga, d)
    la = _log_sigmoid(z) * (1.0 / GLA_GATE_NORMALIZER)
    for ci in range(x_ref.shape[1] // GLA_CHUNK):
        rows = slice(ci * GLA_CHUNK, (ci + 1) * GLA_CHUNK)
        la_ref[0, rows, :kd] = _chunk_cumsum(la[rows, :kd], False)
        la_ref[0, rows, kd:] = _chunk_cumsum(la[rows, kd:], True)
    gb = seg(off_gb, d)
    sga_ref[0] = _sigmoid(ga).astype(BF16)
    r = seg(off_r, d)
    sgb_ref[0] = _sigmoid(gb).astype(BF16)
    u = seg(off_u, d)
    rs_ref[0] = (r * _sigmoid(r)).astype(BF16)
    vs = seg(off_vs, d)
    gu_ref[0] = _gelu(u).astype(BF16)
    v = seg(off_v, d)
    gv = _gelu(vs)
    mu = jnp.mean(gv, axis=-1, keepdims=True)
    cen = gv - mu
    var = jnp.mean(cen * cen, axis=-1, keepdims=True)
    vn_ref[0] = (cen * lax.rsqrt(var + EPS) * lng_ref[...] + lnb_ref[...]).astype(BF16)
    v_ref[0] = v.astype(BF16)


def _in_proj(x, shift, scale, norm_g, w_all, w2cat, b2cat, ln_g, ln_b, *, tm):
    bsz, length, d = x.shape
    kd = d // 2
    hk = kd // GLA_HEADS
    row = lambda b, i: (b, i, 0)
    mod = lambda b, i: (b, 0, 0)
    wide = jax.ShapeDtypeStruct((bsz, length, d), BF16)
    half = jax.ShapeDtypeStruct((bsz, length, kd), F32)
    wide_spec = pl.BlockSpec((1, tm, d), row)
    half_spec = pl.BlockSpec((1, tm, kd), row)
    return pl.pallas_call(
        functools.partial(_inproj_kernel, d=d, kd=kd, q_scale=float(hk) ** -0.5),
        out_shape=(wide,) * 6 + (half, half, jax.ShapeDtypeStruct((bsz, length, 2 * kd), F32)),
        grid=(bsz, length // tm),
        in_specs=[
            pl.BlockSpec((1, tm, d), row),
            pl.BlockSpec((1, 1, d), mod),
            pl.BlockSpec((1, 1, d), mod),
            _const_spec((1, d)),
            _const_spec(w_all.shape),
            _const_spec(w2cat.shape),
            _const_spec(b2cat.shape),
            _const_spec((1, d)),
            _const_spec((1, d)),
        ],
        out_specs=(wide_spec,) * 6 + (half_spec, half_spec, pl.BlockSpec((1, tm, 2 * kd), row)),
        scratch_shapes=[pltpu.VMEM((tm, d), BF16)],
        compiler_params=_params("parallel", "parallel"),
        name="in_proj",
    )(x, shift, scale, norm_g, w_all, w2cat, b2cat, ln_g, ln_b)


def _gla_kernel(qf_ref, kf_ref, laf_ref, vf_ref, qb_ref, kb_ref, lab_ref, vb_ref, s0f_ref, s0b_ref,
                of_ref, ob_ref, sf_ref, sb_ref, *, n_chunks, hk, hv):
    @pl.when(pl.program_id(1) == 0)
    def _():
        sf_ref[...] = s0f_ref[...]
        sb_ref[...] = s0b_ref[...]

    r = lax.broadcasted_iota(jnp.int32, (GLA_CHUNK, GLA_CHUNK), 0)
    c = lax.broadcasted_iota(jnp.int32, (GLA_CHUNK, GLA_CHUNK), 1)
    fwd = (qf_ref, kf_ref, laf_ref, vf_ref, of_ref, sf_ref, c <= r, False)
    bwd = (qb_ref, kb_ref, lab_ref, vb_ref, ob_ref, sb_ref, c >= r, True)

    def decayed_operands(refs, ci):
        q_ref, k_ref, b_ref, _, _, _, _, reverse = refs
        rows = slice(ci * GLA_CHUNK, (ci + 1) * GLA_CHUNK)
        b = b_ref[0, rows, :]
        b_last = b[0:1] if reverse else b[GLA_CHUNK - 1:GLA_CHUNK]
        k = k_ref[0, rows, :]
        qd = (q_ref[0, rows, :] * jnp.exp(b)).astype(BF16)
        ki = (k * jnp.exp(-b)).astype(BF16)
        kdt = (k * jnp.exp(b_last - b)).T.astype(BF16)
        dec = jnp.broadcast_to(jnp.exp(b_last), (SUBLANE, b.shape[1])).T
        return qd, ki, kdt, dec

    def chunks_of(step):
        return ((fwd, step), (bwd, n_chunks - 1 - step))

    ops = [decayed_operands(refs, ci) for refs, ci in chunks_of(0)]
    for step in range(n_chunks):
        streams = [(refs, ci, h, op) for (refs, ci), op in zip(chunks_of(step), ops) for h in range(GLA_HEADS)]
        raw, kvs = [], []
        for (_, _, _, v_ref, _, _, _, _), ci, h, (qd, ki, kdt, _) in streams:
            rows = slice(ci * GLA_CHUNK, (ci + 1) * GLA_CHUNK)
            ks = slice(h * hk, (h + 1) * hk)
            raw.append(_dot_nt(qd[:, ks], ki[:, ks]))
            kvs.append(_dot(kdt[ks, :], v_ref[0, rows, h * hv:(h + 1) * hv]))
        if step + 1 < n_chunks:
            ops = [decayed_operands(refs, ci) for refs, ci in chunks_of(step + 1)]
        for ((_, _, _, v_ref, o_ref, st_ref, keep, _), ci, h, (qd, _, _, dec)), sc, kv in zip(streams, raw, kvs):
            rows = slice(ci * GLA_CHUNK, (ci + 1) * GLA_CHUNK)
            ks = slice(h * hk, (h + 1) * hk)
            vs = slice(h * hv, (h + 1) * hv)
            scores = jnp.where(keep, sc, 0.0).astype(BF16)
            state = st_ref[0, h]
            o_inter = _dot(qd[:, ks], state.astype(BF16))
            o_ref[0, rows, vs] = (_dot(scores, v_ref[0, rows, vs]) + o_inter).astype(BF16)
            st_ref[0, h] = state * jnp.broadcast_to(dec[ks, 0:1], (hk, hv)) + kv


def _gla(q, k, v, la, s0f, s0b, *, tile):
    bsz, length, kd = q.shape
    vd = v.shape[-1]
    hk, hv = kd // GLA_HEADS, vd // GLA_HEADS
    nt = length // tile
    fwd = lambda b, i: (b, i, 0)
    bwd = lambda b, i: (b, nt - 1 - i, 0)
    st_spec = pl.BlockSpec((1, GLA_HEADS, hk, hv), lambda b, i: (b, 0, 0, 0))
    out = jax.ShapeDtypeStruct((bsz, length, vd), BF16)

    def direction_specs(idx, la_col):
        return [
            pl.BlockSpec((1, tile, kd), idx),
            pl.BlockSpec((1, tile, kd), idx),
            pl.BlockSpec((1, tile, kd), lambda b, i: (b, idx(b, i)[1], la_col)),
            pl.BlockSpec((1, tile, vd), idx),
        ]

    return pl.pallas_call(
        functools.partial(_gla_kernel, n_chunks=tile // GLA_CHUNK, hk=hk, hv=hv),
        out_shape=(out, out, jax.ShapeDtypeStruct(s0f.shape, F32), jax.ShapeDtypeStruct(s0b.shape, F32)),
        grid=(bsz, nt),
        in_specs=direction_specs(fwd, 0) + direction_specs(bwd, 1) + [st_spec, st_spec],
        out_specs=(pl.BlockSpec((1, tile, vd), fwd), pl.BlockSpec((1, tile, vd), bwd), st_spec, st_spec),
        compiler_params=_params("parallel", "arbitrary"),
        name="gla_scan",
    )(q, k, la, v, q, k, la, v, s0f, s0b)


def _mix_kernel(x_ref, gu_ref, vn_ref, of_ref, ob_ref, rs_ref, sga_ref, sgb_ref, ws_ref, bs_ref, gn_ref,
                wa_ref, wb_ref, wo_ref, g1_ref, n2_ref, sh2_ref, sc2_ref, x1_ref, h2_ref,
                s_buf, a_buf, b_buf, m_buf, *, tm, hv):
    d = x_ref.shape[-1]
    for n in range(tm // SGU_CHUNK):
        rows = slice(n * SGU_CHUNK, (n + 1) * SGU_CHUNK)
        for g in range(SGU_GROUPS):
            cols = slice(g * LANE, (g + 1) * LANE)
            s_buf[rows, cols] = _dot(ws_ref[g], vn_ref[0, rows, cols]) + bs_ref[:, cols]
    for h in range(GLA_HEADS):
        cols = slice(h * hv, (h + 1) * hv)
        oh = of_ref[0, :, cols].astype(F32) + ob_ref[0, :, cols].astype(F32)
        ms = jnp.mean(oh * oh, axis=-1, keepdims=True)
        on = oh * lax.rsqrt(ms + EPS) * gn_ref[:, cols]
        b_buf[:, cols] = (on * rs_ref[0, :, cols].astype(F32)).astype(BF16)
    proj_b = _dot(b_buf[...], wb_ref[:, :d])
    a_buf[...] = (gu_ref[0].astype(F32) * s_buf[...]).astype(BF16)
    proj_a = _dot(a_buf[...], wa_ref[:, :d])
    m_buf[...] = (sga_ref[0].astype(F32) * proj_a + sgb_ref[0].astype(F32) * proj_b).astype(BF16)
    x1 = x_ref[0] + g1_ref[0] * _dot(m_buf[...], wo_ref[:, :d])
    x1_ref[0] = x1
    h2_ref[0] = _rms_mod(x1, n2_ref[...], sh2_ref[0], sc2_ref[0]).astype(BF16)


def _mix(x, gu, vn, o_f, o_b, rs, sga, sgb, ws, bs_full, gn_g, wa, wb, wo, g1, n2_g, sh2, sc2, *, tm):
    bsz, length, d = x.shape
    row = lambda b, i: (b, i, 0)
    mod = lambda b, i: (b, 0, 0)
    tile = pl.BlockSpec((1, tm, d), row)
    modv = pl.BlockSpec((1, 1, d), mod)
    return pl.pallas_call(
        functools.partial(_mix_kernel, tm=tm, hv=d // GLA_HEADS),
        out_shape=(jax.ShapeDtypeStruct((bsz, length, d), F32), jax.ShapeDtypeStruct((bsz, length, d), BF16)),
        grid=(bsz, length // tm),
        in_specs=[tile] * 8 + [
            _const_spec(ws.shape),
            _const_spec(bs_full.shape),
            _const_spec((1, d)),
            _const_spec(wa.shape),
            _const_spec(wb.shape),
            _const_spec(wo.shape),
            modv,
            _const_spec((1, d)),
            modv,
            modv,
        ],
        out_specs=(tile, tile),
        scratch_shapes=[pltpu.VMEM((tm, d), F32)] + [pltpu.VMEM((tm, d), BF16) for _ in range(3)],
        compiler_params=_params("parallel", "parallel"),
        name="mix",
    )(x, gu, vn, o_f, o_b, rs, sga, sgb, ws, bs_full, gn_g, wa, wb, wo, g1, n2_g, sh2, sc2)


def _ffn_kernel(h_ref, x_ref, wu_ref, cw_ref, cb_ref, wd_ref, g2_ref, fg_ref, o_ref,
                a0_ref, a1_ref, val0_ref, val1_ref, above_ref, *gated_refs,
                n_tiles, tile, img_w, has_rows, final_norm):
    s = pl.program_id(1)
    dff = a0_ref.shape[-1]
    d = x_ref.shape[-1]
    blk = gated_refs[0].shape[0]
    rows_img = tile // img_w
    bufs = ((a0_ref, val0_ref), (a1_ref, val1_ref))

    sub = lax.broadcasted_iota(jnp.int32, (SUBLANE, LANE), 0)
    not_first = sub >= 1
    not_last = sub <= SUBLANE - 2

    def up_pieces(a_dst, val_dst):
        def piece(rows, col, dst, cast):
            def run():
                res = _dot(h_ref[0, rows, :], wu_ref[:, col:col + UP_SLAB])
                dst[rows, col % dff:col % dff + UP_SLAB] = res.astype(BF16) if cast else res
            return run
        pieces = []
        for j in range(tile // blk):
            rows = slice(j * blk, (j + 1) * blk)
            pieces += [piece(rows, col, a_dst, False) for col in range(0, dff, UP_SLAB)]
            pieces += [piece(rows, dff + col, val_dst, True) for col in range(0, dff, UP_SLAB)]
        return pieces

    def conv_strip(cur, val, below, r, c, gated, out_rows):
        tok = slice(r * img_w, (r + 1) * img_w)
        cols = slice(c * LANE, (c + 1) * LANE)
        srcs = [(1, cur, tok)]
        if has_rows:
            srcs.append((0, cur, slice((r - 1) * img_w, r * img_w)) if r > 0 else (0, above_ref, slice(0, img_w)))
            if r < rows_img - 1:
                srcs.append((2, cur, slice((r + 1) * img_w, (r + 2) * img_w)))
            elif below is not None:
                srcs.append((2, below, slice(0, img_w)))
        acc = [None, None, None]
        for kr, ref, rws in srcs:
            src = ref[rws, cols]
            for dc in (-1, 0, 1):
                tap = cw_ref[kr * CONV_K + dc + 1:kr * CONV_K + dc + 2, cols]
                acc[dc + 1] = tap * src if acc[dc + 1] is None else acc[dc + 1] + tap * src
        n_sl = img_w // SUBLANE
        dn = [pltpu.roll(acc[0][j * SUBLANE:(j + 1) * SUBLANE], 1, 0) for j in range(n_sl)]
        upw = [pltpu.roll(acc[2][j * SUBLANE:(j + 1) * SUBLANE], SUBLANE - 1, 0) for j in range(n_sl)]
        left = jnp.concatenate(
            [jnp.where(not_first, dn[j], dn[j - 1] if j > 0 else 0.0) for j in range(n_sl)], axis=0)
        right = jnp.concatenate(
            [jnp.where(not_last, upw[j], upw[j + 1] if j < n_sl - 1 else 0.0) for j in range(n_sl)], axis=0)
        conv = acc[1] + left + right + cb_ref[:, cols]
        gated[out_rows, cols] = (_gelu(conv) * val[tok, cols].astype(F32)).astype(BF16)

    def conv_down(cur, val, below, pieces=()):
        slabs = dff // UP_SLAB
        total = len(gated_refs) * slabs
        assert not pieces or len(pieces) == 2 * total
        done = issued = 0
        for j, gated in enumerate(gated_refs):
            y = None
            for k in range(slabs):
                done += 1
                while issued * total < done * len(pieces):
                    pieces[issued]()
                    issued += 1
                for rr in range(blk // img_w):
                    for c in range(k * (UP_SLAB // LANE), (k + 1) * (UP_SLAB // LANE)):
                        conv_strip(cur, val, below, j * (blk // img_w) + rr, c, gated,
                                   slice(rr * img_w, (rr + 1) * img_w))
                ks = slice(k * UP_SLAB, (k + 1) * UP_SLAB)
                part = _dot(gated[:, ks], wd_ref[ks, :d])
                y = part if y is None else y + part
            rows = slice(j * blk, (j + 1) * blk)
            x2 = x_ref[0, rows, :] + g2_ref[0] * y
            if final_norm:
                ms = jnp.mean(x2 * x2, axis=-1, keepdims=True)
                x2 = x2 * lax.rsqrt(ms + EPS) * fg_ref[...]
            o_ref[0, rows, :] = x2
        if has_rows:
            above_ref[...] = cur[tile - img_w:tile, :]

    @pl.when(s == 0)
    def _():
        if has_rows:
            above_ref[...] = jnp.zeros(above_ref.shape, F32)
        for run in up_pieces(*bufs[0]):
            run()

    for parity in sorted({step % 2 for step in range(1, n_tiles)}):
        @pl.when(jnp.logical_and(jnp.logical_and(s > 0, s < n_tiles), lax.rem(s, 2) == parity))
        def _():
            new, (cur, val) = bufs[parity], bufs[1 - parity]
            conv_down(cur, val, new[0], up_pieces(*new))

    @pl.when(s == n_tiles)
    def _():
        conv_down(*bufs[(n_tiles - 1) % 2], None)


def _ffn(h2, x1, w_up, conv_w, conv_b, w_down, g2, final_g, *, tile, img_w, final_norm, single_row_images=False):
    bsz, length, d = x1.shape
    dff = w_down.shape[0]
    n_tiles = length // tile
    assert not single_row_images or tile == img_w
    has_rows = length > img_w and not single_row_images
    blk = max(FFN_DOWN_BLOCK, img_w)
    up_tile = lambda b, s: (b, jnp.minimum(s, n_tiles - 1), 0)
    down_tile = lambda b, s: (b, jnp.maximum(s - 1, 0), 0)
    return pl.pallas_call(
        functools.partial(_ffn_kernel, n_tiles=n_tiles, tile=tile, img_w=img_w, has_rows=has_rows,
                          final_norm=final_norm),
        out_shape=jax.ShapeDtypeStruct(x1.shape, F32),
        grid=(bsz, n_tiles + 1),
        in_specs=[
            pl.BlockSpec((1, tile, d), up_tile),
            pl.BlockSpec((1, tile, d), down_tile),
            _const_spec(w_up.shape),
            _const_spec(conv_w.shape),
            _const_spec(conv_b.shape),
            _const_spec(w_down.shape),
            pl.BlockSpec((1, 1, d), lambda b, s: (b, 0, 0)),
            _const_spec((1, d)),
        ],
        out_specs=pl.BlockSpec((1, tile, d), down_tile),
        scratch_shapes=[pltpu.VMEM((tile, dff), F32), pltpu.VMEM((tile, dff), F32),
                        pltpu.VMEM((tile, dff), BF16), pltpu.VMEM((tile, dff), BF16),
                        pltpu.VMEM((img_w, dff), F32)]
        + [pltpu.VMEM((blk, dff), BF16) for _ in range(tile // blk)],
        compiler_params=_params("parallel", "arbitrary"),
        name="ffn",
    )(h2, x1, w_up, conv_w, conv_b, w_down, g2, final_g)


def _stream_layer(x, mods, wts, s0f, s0b, *, tm, gla_tile, ffn_tile, img_w, states_only, final_norm,
                  shared_mods=False):
    sh1, sc1, g1, sh2, sc2, g2 = mods
    gu, vn, v, rs, sga, sgb, q, k, la = _in_proj(
        x, sh1, sc1, wts["norm1_g"], wts["w_all"], wts["w2cat"], wts["b2cat"],
        wts["sgu_ln_g"], wts["sgu_ln_b"], tm=tm)
    o_f, o_b, s_f, s_b = _gla(q, k, v, la, s0f, s0b, tile=gla_tile)
    if states_only:
        return None, s_f, s_b
    x1, h2 = _mix(x, gu, vn, o_f, o_b, rs, sga, sgb, wts["sgu_w"], wts["sgu_b_full"], wts["gla_norm_g"],
                  wts["w_br_a"], wts["w_br_b"], wts["w_o"], g1, wts["norm2_g"], sh2, sc2, tm=tm)
    ffn_w = (wts["ffn_w_up"], wts["ffn_conv_w"], wts["ffn_conv_b"], wts["ffn_w_down"])
    if shared_mods and x.shape[1] == img_w:
        flat = (1, x.shape[0] * img_w, x.shape[2])
        x2 = _ffn(h2.reshape(flat), x1.reshape(flat), *ffn_w, g2[:1], wts["final_g"], tile=img_w, img_w=img_w,
                  final_norm=final_norm, single_row_images=True).reshape(x.shape)
    else:
        x2 = _ffn(h2, x1, *ffn_w, g2, wts["final_g"], tile=ffn_tile, img_w=img_w, final_norm=final_norm)
    return x2, s_f, s_b


def kernel(x, c, ctx, c_ctx, w_ada, b_ada, norm1_g, norm2_g, w_in, sgu_ln_g, sgu_ln_b, sgu_w, sgu_b, gla_w2, gla_b2, gla_norm_g, w_br_a, w_br_b, w_o, ffn_w_up, ffn_conv_w, ffn_conv_b, ffn_w_down, final_norm_g):
    bsz, seq, d = x.shape
    ctx_len = ctx.shape[1]
    depth = w_in.shape[0]
    kd = d // 2
    hk, hv = kd // GLA_HEADS, d // GLA_HEADS
    dff = ffn_w_down.shape[1]

    n_rows = -(-(bsz + 1) // SUBLANE) * SUBLANE
    cc = jnp.zeros((n_rows, d), F32).at[:bsz].set(c).at[bsz].set(c_ctx)
    mods = _ada_mods(cc, w_ada, b_ada)

    w_all = _bf16_odd_tiles(w_in)
    w2cat = jnp.zeros((depth, LANE, 2 * kd), F32)
    w2cat = w2cat.at[:, :GLA_RANK, :kd].set(gla_w2[:, 0]).at[:, GLA_RANK:2 * GLA_RANK, kd:].set(gla_w2[:, 1])
    w2cat = w2cat.astype(BF16)
    b2cat = gla_b2.reshape(depth, 1, 2 * kd)
    sgu_b_full = jnp.repeat(jnp.swapaxes(sgu_b, 1, 2), LANE, axis=2)

    zero_state = jnp.zeros((bsz, GLA_HEADS, hk, hv), F32)
    tm = 512
    for l in range(depth):
        last = l == depth - 1
        wts = {
            "norm1_g": norm1_g[l][None], "norm2_g": norm2_g[l][None],
            "w_all": w_all[l], "w2cat": w2cat[l], "b2cat": b2cat[l],
            "sgu_ln_g": sgu_ln_g[l][None], "sgu_ln_b": sgu_ln_b[l][None],
            "sgu_w": sgu_w[l].astype(BF16), "sgu_b_full": sgu_b_full[l],
            "gla_norm_g": gla_norm_g[l][None],
            "w_br_a": _bf16_odd_tiles(w_br_a[l]), "w_br_b": _bf16_odd_tiles(w_br_b[l]),
            "w_o": _bf16_odd_tiles(w_o[l]),
            "ffn_w_up": ffn_w_up[l].astype(BF16), "ffn_conv_w": ffn_conv_w[l].reshape(CONV_K * CONV_K, dff),
            "ffn_conv_b": ffn_conv_b[l][None], "ffn_w_down": _bf16_odd_tiles(ffn_w_down[l]),
            "final_g": final_norm_g[None],
        }
        m = mods[l].reshape(n_rows, N_MOD, d)
        mods_x = [m[:bsz, j][:, None, :] for j in range(N_MOD)]
        mods_c = [jnp.broadcast_to(m[bsz, j][None, None, :], (bsz, 1, d)) for j in range(N_MOD)]
        ctx, s_f, s_b = _stream_layer(
            ctx, mods_c, wts, zero_state, zero_state, tm=min(tm, ctx_len), gla_tile=ctx_len, ffn_tile=ctx_len,
            img_w=ctx_len, states_only=last, final_norm=False, shared_mods=True)
        x, _, _ = _stream_layer(
            x, mods_x, wts, s_f, s_b, tm=tm, gla_tile=1024, ffn_tile=256, img_w=GRID_W,
            states_only=False, final_norm=last)
    return x
```
